```python
import jax
import jax.numpy as jnp
from jax import lax
import numpy as np

D_MODEL = 2048
BATCH = 2
SEQ = 4096
DEPTH = 2

GRID_W = 64
CTX_LEN = 256
HEAD_DIM = 128
N_Q_HEADS = 8
N_KV_HEADS = 2
Q_PER_KV = N_Q_HEADS // N_KV_HEADS
ATTN_WIDTH = N_Q_HEADS * HEAD_DIM
KV_WIDTH = N_KV_HEADS * HEAD_DIM
ROPE_AXIS_DIM = HEAD_DIM // 2
ROPE_THETA = 10000.0
Q_BLOCK = 128
FOURIER_GROUPS = 8
FOURIER_GROUP_DIM = 128
FOURIER_WIDTH = FOURIER_GROUPS * FOURIER_GROUP_DIM
AB_IN_WIDTH = ATTN_WIDTH + 2 * KV_WIDTH + FOURIER_WIDTH
AB_OUT_WIDTH = ATTN_WIDTH + FOURIER_WIDTH
CONV_WIDTH = D_MODEL
CONV_WINDOW = 31
N_EXPERTS = 32
TOP_K = 4
EXPERT_FF = D_MODEL // 2
SWIGLU_LIMIT = 7.0
SWIGLU_ALPHA = 1.702
MOE_BLOCK = 128
N_MOD = 6
EPS = 1e-6
N_AB_LAYERS = (DEPTH + 1) // 2
N_CONV_LAYERS = DEPTH // 2

kernel_name = 'hybrid_dit_attn_fnet_conformer_moe'


def rms_norm(x, g):
    xf = x.astype(jnp.float32)
    y = xf * lax.rsqrt(jnp.mean(xf * xf, axis=-1, keepdims=True) + EPS)
    return (y * g.astype(jnp.float32)).astype(x.dtype)


def layer_norm(x, g, b):
    xf = x.astype(jnp.float32)
    mu = jnp.mean(xf, axis=-1, keepdims=True)
    var = jnp.mean(jnp.square(xf - mu), axis=-1, keepdims=True)
    y = (xf - mu) * lax.rsqrt(var + EPS)
    return (y * g.astype(jnp.float32) + b.astype(jnp.float32)).astype(x.dtype)


def adaln_params(cond, w_mod, b_mod):
    m = jax.nn.silu(cond) @ w_mod + b_mod
    return jnp.split(m, N_MOD, axis=-1)


def modulate(h, shift, scale):
    return h * (1 + scale) + shift


def axial_rope_tables(n_tokens):
    rows = n_tokens // GRID_W
    row = jnp.repeat(jnp.arange(rows, dtype=jnp.float32), GRID_W)
    col = jnp.tile(jnp.arange(GRID_W, dtype=jnp.float32), rows)
    inv_freq = ROPE_THETA ** (-jnp.arange(0, ROPE_AXIS_DIM, 2, dtype=jnp.float32) / ROPE_AXIS_DIM)
    ang_r = row[:, None] * inv_freq
    ang_c = col[:, None] * inv_freq
    return (jnp.cos(ang_r), jnp.sin(ang_r), jnp.cos(ang_c), jnp.sin(ang_c))


def rotate_axis(x, cos, sin):
    x1, x2 = jnp.split(x, 2, axis=-1)
    cos = cos[:, None, :]
    sin = sin[:, None, :]
    return jnp.concatenate([x1 * cos - x2 * sin, x1 * sin + x2 * cos], axis=-1)


def apply_axial_rope(x, tables):
    cos_r, sin_r, cos_c, sin_c = tables
    xf = x.astype(jnp.float32)
    out = jnp.concatenate([rotate_axis(xf[..., :ROPE_AXIS_DIM], cos_r, sin_r),
                           rotate_axis(xf[..., ROPE_AXIS_DIM:], cos_c, sin_c)], axis=-1)
    return out.astype(x.dtype)


def block_attention(q, k, v):
    b, lq = q.shape[0], q.shape[1]
    nb = lq // Q_BLOCK
    qb = jnp.moveaxis(q.reshape(b, nb, Q_BLOCK, N_KV_HEADS, Q_PER_KV, HEAD_DIM), 1, 0)
    scale = HEAD_DIM ** -0.5

    def one_block(qi):
        s = jnp.einsum('bqhgd,bkhd->bhgqk', qi, k, preferred_element_type=jnp.float32) * scale
        p = jax.nn.softmax(s, axis=-1).astype(v.dtype)
        return jnp.einsum('bhgqk,bkhd->bqhgd', p, v)

    o = lax.map(one_block, qb)
    return jnp.moveaxis(o, 0, 1).reshape(b, lq, ATTN_WIDTH)


def fourier_mix(u):
    b, l, _ = u.shape
    ug = u.reshape(b, l, FOURIER_GROUPS, FOURIER_GROUP_DIM).astype(jnp.float32)
    y = jnp.fft.fft2(ug, axes=(1, 3), norm='ortho').real
    return y.reshape(b, l, FOURIER_WIDTH).astype(u.dtype)


def ab_mixer(h_lat, h_ctx, w_in, q_gain, k_gain, w_out, tables, ctx_live):
    b, l, _ = h_lat.shape
    cl = h_ctx.shape[1]
    cuts = [ATTN_WIDTH, ATTN_WIDTH + KV_WIDTH, ATTN_WIDTH + 2 * KV_WIDTH]
    q, k, v, f = jnp.split(h_lat @ w_in, cuts, axis=-1)
    q = apply_axial_rope(rms_norm(q.reshape(b, l, N_Q_HEADS, HEAD_DIM), q_gain), tables)
    k = apply_axial_rope(rms_norm(k.reshape(b, l, N_KV_HEADS, HEAD_DIM), k_gain), tables)
    v = v.reshape(b, l, N_KV_HEADS, HEAD_DIM)
    if ctx_live:
        qc, kc, vc, fc = jnp.split(h_ctx @ w_in, cuts, axis=-1)
    else:
        kc, vc = jnp.split(h_ctx @ w_in[:, ATTN_WIDTH:ATTN_WIDTH + 2 * KV_WIDTH], 2, axis=-1)
    kc = rms_norm(kc.reshape(b, cl, N_KV_HEADS, HEAD_DIM), k_gain)
    vc = vc.reshape(b, cl, N_KV_HEADS, HEAD_DIM)
    q = q.reshape(b, l, N_KV_HEADS, Q_PER_KV, HEAD_DIM)
    attn = block_attention(q, jnp.concatenate([k, kc], axis=1), jnp.concatenate([v, vc], axis=1))
    lat_out = jnp.concatenate([attn, fourier_mix(f)], axis=-1) @ w_out
    if not ctx_live:
        return lat_out, None
    qc = rms_norm(qc.reshape(b, cl, N_Q_HEADS, HEAD_DIM), q_gain).reshape(b, cl, N_KV_HEADS, Q_PER_KV, HEAD_DIM)
    attn_c = block_attention(qc, kc, vc)
    ctx_out = jnp.concatenate([attn_c, fourier_mix(fc)], axis=-1) @ w_out
    return lat_out, ctx_out


def conformer_conv(h, w_in, b_in, w_dw, b_dw, ln_g, ln_b, w_out, b_out):
    a, gate = jnp.split(h @ w_in + b_in, 2, axis=-1)
    u = a * jax.nn.sigmoid(gate)
    pad = CONV_WINDOW // 2
    u = lax.conv_general_dilated(u, w_dw[:, None, :].astype(u.dtype), window_strides=(1,),
                                 padding=[(pad, pad)], dimension_numbers=('NWC', 'WIO', 'NWC'),
                                 feature_group_count=CONV_WIDTH) + b_dw
    u = jax.nn.silu(layer_norm(u, ln_g, ln_b))
    return u @ w_out + b_out


def moe(x, w_router, b_router, w1, b1, w2, b2):
    n, d = x.shape
    logits = (x @ w_router + b_router).astype(jnp.float32)
    top_val, top_idx = lax.top_k(logits, TOP_K)
    gates = jax.nn.softmax(top_val, axis=-1)
    flat_e = top_idx.reshape(-1)
    flat_tok = jnp.repeat(jnp.arange(n, dtype=jnp.int32), TOP_K)
    flat_w = gates.reshape(-1)
    order = jnp.argsort(flat_e)
    sorted_e = flat_e[order]
    counts = jnp.bincount(flat_e, length=N_EXPERTS)
    padded = (counts + MOE_BLOCK - 1) // MOE_BLOCK * MOE_BLOCK
    pad_end = jnp.cumsum(padded)
    pad_start = pad_end - padded
    start = jnp.cumsum(counts) - counts
    rank = jnp.arange(n * TOP_K) - start[sorted_e]
    dest = pad_start[sorted_e] + rank
    n_rows = (n * TOP_K + MOE_BLOCK - 1) // MOE_BLOCK * MOE_BLOCK + N_EXPERTS * MOE_BLOCK
    n_blocks = n_rows // MOE_BLOCK
    row_tok = jnp.full((n_rows,), n, jnp.int32).at[dest].set(flat_tok[order])
    row_w = jnp.zeros((n_rows,), jnp.float32).at[dest].set(flat_w[order])
    block_e = jnp.minimum(jnp.searchsorted(pad_end, jnp.arange(n_blocks) * MOE_BLOCK, side='right'),
                          N_EXPERTS - 1)
    x_pad = jnp.concatenate([x, jnp.zeros((1, d), x.dtype)], axis=0)
    xb = x_pad[row_tok].reshape(n_blocks, MOE_BLOCK, d)

    def expert_block(args):
        xi, e = args
        h = xi @ w1[e] + b1[e]
        g = jnp.minimum(h[..., ::2], SWIGLU_LIMIT)
        lin = jnp.clip(h[..., 1::2], -SWIGLU_LIMIT, SWIGLU_LIMIT)
        act = g * jax.nn.sigmoid(SWIGLU_ALPHA * g) * (lin + 1)
        return act @ w2[e] + b2[e]

    yb = lax.map(expert_block, (xb, block_e))
    y = yb.reshape(n_rows, d) * row_w[:, None].astype(yb.dtype)
    return jax.ops.segment_sum(y, row_tok, num_segments=n + 1)[:n]


def setup_inputs(seed: int = 0) -> dict:
    key = jax.random.key(seed)
    ks = iter(jax.random.split(key, 40))

    def nrm(shape, scale):
        return jax.random.normal(next(ks), shape, jnp.float32) * scale

    def gain(shape):
        return 1.0 + nrm(shape, 0.02)

    d = D_MODEL
    return {
        'x': nrm((BATCH, SEQ, d), 1.0),
        'c': nrm((BATCH, d), 1.0),
        'ctx': nrm((BATCH, CTX_LEN, d), 1.0),
        'c_ctx': nrm((d,), 1.0),
        'w_mod': nrm((DEPTH, d, N_MOD * d), 0.5 * d ** -0.5),
        'b_mod': nrm((DEPTH, N_MOD * d), 0.02),
        'g_mix': gain((DEPTH, d)),
        'g_ffn': gain((DEPTH, d)),
        'ab_w_in': nrm((N_AB_LAYERS, d, AB_IN_WIDTH), d ** -0.5),
        'ab_q_gain': gain((N_AB_LAYERS, HEAD_DIM)),
        'ab_k_gain': gain((N_AB_LAYERS, HEAD_DIM)),
        'ab_w_out': nrm((N_AB_LAYERS, AB_OUT_WIDTH, d), AB_OUT_WIDTH ** -0.5),
        'cv_w_in': nrm((N_CONV_LAYERS, d, 2 * CONV_WIDTH), d ** -0.5),
        'cv_b_in': nrm((N_CONV_LAYERS, 2 * CONV_WIDTH), 0.02),
        'cv_w_dw': nrm((N_CONV_LAYERS, CONV_WINDOW, CONV_WIDTH), CONV_WINDOW ** -0.5),
        'cv_b_dw': nrm((N_CONV_LAYERS, CONV_WIDTH), 0.02),
        'cv_ln_g': gain((N_CONV_LAYERS, CONV_WIDTH)),
        'cv_ln_b': nrm((N_CONV_LAYERS, CONV_WIDTH), 0.02),
        'cv_w_out': nrm((N_CONV_LAYERS, CONV_WIDTH, d), CONV_WIDTH ** -0.5),
        'cv_b_out': nrm((N_CONV_LAYERS, d), 0.02),
        'moe_w_router': nrm((DEPTH, d, N_EXPERTS), d ** -0.5),
        'moe_b_router': nrm((DEPTH, N_EXPERTS), 0.01),
        'moe_w1': nrm((DEPTH, N_EXPERTS, d, 2 * EXPERT_FF), d ** -0.5),
        'moe_b1': nrm((DEPTH, N_EXPERTS, 2 * EXPERT_FF), 0.02),
        'moe_w2': nrm((DEPTH, N_EXPERTS, EXPERT_FF, d), EXPERT_FF ** -0.5),
        'moe_b2': nrm((DEPTH, N_EXPERTS, d), 0.02),
        'g_final': gain((d,)),
    }


def reference(x, c, ctx, c_ctx, w_mod, b_mod, g_mix, g_ffn, ab_w_in, ab_q_gain, ab_k_gain, ab_w_out,
              cv_w_in, cv_b_in, cv_w_dw, cv_b_dw, cv_ln_g, cv_ln_b, cv_w_out, cv_b_out,
              moe_w_router, moe_b_router, moe_w1, moe_b1, moe_w2, moe_b2, g_final):
    b, l, d = x.shape
    tables = axial_rope_tables(l)
    z = ctx
    for i in range(DEPTH):
        is_ab = i % 2 == 0
        j = i // 2
        ctx_live = any(k % 2 == 0 for k in range(i + 1, DEPTH))
        sh1, sc1, gt1, sh2, sc2, gt2 = adaln_params(c, w_mod[i], b_mod[i])
        h_lat = modulate(rms_norm(x, g_mix[i]), sh1[:, None], sc1[:, None])
        if is_ab or ctx_live:
            cm = adaln_params(c_ctx, w_mod[i], b_mod[i])
            h_ctx = modulate(rms_norm(z, g_mix[i]), cm[0], cm[1])
        if is_ab:
            mix_lat, mix_ctx = ab_mixer(h_lat, h_ctx, ab_w_in[j], ab_q_gain[j], ab_k_gain[j],
                                        ab_w_out[j], tables, ctx_live)
        else:
            conv_args = (cv_w_in[j], cv_b_in[j], cv_w_dw[j], cv_b_dw[j], cv_ln_g[j], cv_ln_b[j],
                         cv_w_out[j], cv_b_out[j])
            mix_lat = conformer_conv(h_lat, *conv_args)
            mix_ctx = conformer_conv(h_ctx, *conv_args) if ctx_live else None
        x = x + gt1[:, None] * mix_lat
        tokens = modulate(rms_norm(x, g_ffn[i]), sh2[:, None], sc2[:, None]).reshape(b * l, d)
        if ctx_live:
            z = z + cm[2] * mix_ctx
            f_ctx = modulate(rms_norm(z, g_ffn[i]), cm[3], cm[4]).reshape(-1, d)
            tokens = jnp.concatenate([tokens, f_ctx], axis=0)
        y = moe(tokens, moe_w_router[i], moe_b_router[i], moe_w1[i], moe_b1[i], moe_w2[i], moe_b2[i])
        x = x + gt2[:, None] * y[:b * l].reshape(b, l, d)
        if ctx_live:
            z = z + cm[5] * y[b * l:].reshape(b, -1, d)
    return rms_norm(x, g_final)
```

```python
import functools

import numpy as np
import jax
import jax.numpy as jnp
from jax import lax
from jax.experimental import pallas as pl
from jax.experimental.pallas import tpu as pltpu

F32 = jnp.float32
BF16 = jnp.bfloat16
HIGHEST = lax.Precision.HIGHEST

LANES = 128
SUBLANES = 8
HEAD_DIM = 128
N_Q_HEADS = 8
N_KV_HEADS = 2
Q_PER_KV = N_Q_HEADS // N_KV_HEADS
ATTN_WIDTH = N_Q_HEADS * HEAD_DIM
KV_WIDTH = N_KV_HEADS * HEAD_DIM
GRID_W = 64
ROPE_AXIS_DIM = HEAD_DIM // 2
ROPE_THETA = 10000.0
FOURIER_GROUPS = 8
FOURIER_GROUP_DIM = 128
FOURIER_WIDTH = FOURIER_GROUPS * FOURIER_GROUP_DIM
TOP_K = 4
SWIGLU_LIMIT = 7.0
SWIGLU_ALPHA = 1.702
N_MOD = 6
EPS = 1e-6
MOD_ROWS = 8
HALO = 16
MIB = 1024 * 1024


def _params(sem, vmem_mib):
    return pltpu.CompilerParams(dimension_semantics=sem, vmem_limit_bytes=vmem_mib * MIB)


def _row_tile(n, want):
    t = min(n, want)
    assert n % t == 0, (n, t)
    return t


def _resident(shape, index_map):
    return pl.BlockSpec(shape, index_map, pipeline_mode=pl.Buffered(1))


def _mod_kernel(c_ref, w_ref, b_ref, o_ref):
    c = c_ref[...]
    s = c * jax.nn.sigmoid(c)
    o_ref[...] = jnp.dot(s, w_ref[...], preferred_element_type=F32, precision=HIGHEST) + b_ref[...]


def _modulation(cond, w_mod, b_mod):
    depth, d, n = w_mod.shape
    tn = _row_tile(n, min(1024, d))
    return pl.pallas_call(
        _mod_kernel,
        grid=(depth, n // tn),
        in_specs=[
            pl.BlockSpec((MOD_ROWS, d), lambda i, j: (0, 0)),
            pl.BlockSpec((None, d, tn), lambda i, j: (i, 0, j)),
            pl.BlockSpec((None, 1, tn), lambda i, j: (i, 0, j)),
        ],
        out_specs=pl.BlockSpec((None, MOD_ROWS, tn), lambda i, j: (i, 0, j)),
        out_shape=jax.ShapeDtypeStruct((depth, MOD_ROWS, n), F32),
        compiler_params=_params(("arbitrary", "arbitrary"), 40),
        name="adaln_modulation",
    )(cond, w_mod, b_mod.reshape(depth, 1, n))


def _norm_mod(x, a, b):
    ms = jnp.mean(x * x, axis=-1, keepdims=True)
    return x * lax.rsqrt(ms + EPS) * a + b


def _to_slab(ref, val, rows, p):
    for j in range(p):
        ref[pl.ds(j, rows, stride=p), :] = val[:, j * LANES:(j + 1) * LANES]


def _from_slab(ref, rows, p):
    return jnp.concatenate([ref[pl.ds(j, rows, stride=p), :] for j in range(p)], axis=1)


def _residual_epilogue(mix, x_ref, gate_ref, a2_ref, b2_ref, wr_ref, br_ref, xo_ref, tok_ref, lg_ref):
    tm, d = x_ref.shape
    xn = x_ref[...] + gate_ref[...] * mix
    xo_ref[...] = xn
    tok = _norm_mod(xn, a2_ref[...], b2_ref[...])
    lg_ref[...] = jnp.dot(tok, wr_ref[...], preferred_element_type=F32, precision=HIGHEST) + br_ref[...]
    _to_slab(tok_ref, tok, tm, d // LANES)


def _epilogue_specs(tm, d, n_exp, tiles_per_batch):
    bmap = lambda i: (i // tiles_per_batch, 0, 0)
    in_specs = [
        pl.BlockSpec((tm, d), lambda i: (i, 0)),
        pl.BlockSpec((None, 1, d), bmap),
        pl.BlockSpec((None, 1, d), bmap),
        pl.BlockSpec((None, 1, d), bmap),
        _resident((d, n_exp), lambda i: (0, 0)),
        _resident((1, n_exp), lambda i: (0, 0)),
    ]
    p = d // LANES
    out_specs = [
        pl.BlockSpec((tm, d), lambda i: (i, 0)),
        pl.BlockSpec((tm * p, LANES), lambda i: (i, 0)),
        pl.BlockSpec((tm, n_exp), lambda i: (i, 0)),
    ]
    return in_specs, out_specs


def _epilogue_out_shape(n, d, n_exp):
    return [
        jax.ShapeDtypeStruct((n, d), F32),
        jax.ShapeDtypeStruct((n * (d // LANES), LANES), F32),
        jax.ShapeDtypeStruct((n, n_exp), F32),
    ]


def _ab_in_kernel(x_ref, a_ref, b_ref, w_ref, qg_ref, kg_ref, cos_ref, sin_ref,
                  q_ref, k_ref, v_ref, f_ref):
    h = _norm_mod(x_ref[...], a_ref[...], b_ref[...]).astype(BF16)
    y = jnp.dot(h, w_ref[...], preferred_element_type=F32)
    cos = cos_ref[...]
    sin = sin_ref[...]
    lane = lax.broadcasted_iota(jnp.int32, cos.shape, 1)
    low = (lane % (2 * (ROPE_AXIS_DIM // 2))) < (ROPE_AXIS_DIM // 2)

    def head(col, gain, scale):
        blk = y[:, col:col + HEAD_DIM]
        ms = jnp.mean(blk * blk, axis=-1, keepdims=True)
        r = blk * lax.rsqrt(ms + EPS) * gain
        up = pltpu.roll(r, HEAD_DIM - ROPE_AXIS_DIM // 2, axis=1)
        dn = pltpu.roll(r, ROPE_AXIS_DIM // 2, axis=1)
        return (r * cos + jnp.where(low, up, dn) * sin) * scale

    for hd in range(N_Q_HEADS):
        q_ref[:, hd * HEAD_DIM:(hd + 1) * HEAD_DIM] = head(
            hd * HEAD_DIM, qg_ref[...], HEAD_DIM ** -0.5).astype(BF16)
    for hd in range(N_KV_HEADS):
        k_ref[:, hd * HEAD_DIM:(hd + 1) * HEAD_DIM] = head(
            ATTN_WIDTH + hd * HEAD_DIM, kg_ref[...], 1.0).astype(BF16)
    v_ref[...] = y[:, ATTN_WIDTH + KV_WIDTH:ATTN_WIDTH + 2 * KV_WIDTH].astype(BF16)
    f_ref[...] = y[:, ATTN_WIDTH + 2 * KV_WIDTH:].astype(BF16)


def _ab_in(xin, a_mod, b_mod, w_in, q_gain, k_gain, cos_t, sin_t, seq, tm):
    bsz, lt, d = xin.shape
    n_out = w_in.shape[1]
    lat_tiles = seq // tm
    mmap = lambda b, t: (jnp.where(t < lat_tiles, b, bsz), 0, 0)
    row = lambda b, t: (b, t, 0)
    return pl.pallas_call(
        _ab_in_kernel,
        grid=(bsz, lt // tm),
        in_specs=[
            pl.BlockSpec((None, tm, d), row),
            pl.BlockSpec((None, 1, d), mmap),
            pl.BlockSpec((None, 1, d), mmap),
            _resident((d, n_out), lambda b, t: (0, 0)),
            _resident((1, HEAD_DIM), lambda b, t: (0, 0)),
            _resident((1, HEAD_DIM), lambda b, t: (0, 0)),
            pl.BlockSpec((tm, HEAD_DIM), lambda b, t: (t, 0)),
            pl.BlockSpec((tm, HEAD_DIM), lambda b, t: (t, 0)),
        ],
        out_specs=[
            pl.BlockSpec((None, tm, ATTN_WIDTH), row),
            pl.BlockSpec((None, tm, KV_WIDTH), row),
            pl.BlockSpec((None, tm, KV_WIDTH), row),
            pl.BlockSpec((None, tm, FOURIER_WIDTH), row),
        ],
        out_shape=[
            jax.ShapeDtypeStruct((bsz, lt, ATTN_WIDTH), BF16),
            jax.ShapeDtypeStruct((bsz, lt, KV_WIDTH), BF16),
            jax.ShapeDtypeStruct((bsz, lt, KV_WIDTH), BF16),
            jax.ShapeDtypeStruct((bsz, lt, FOURIER_WIDTH), BF16),
        ],
        compiler_params=_params(("arbitrary", "arbitrary"), 48),
        name="ab_in_proj",
    )(xin, a_mod, b_mod, w_in, q_gain, k_gain, cos_t, sin_t)


def _rope_tables(seq, ctx_len):
    pos = np.arange(seq)
    inv_freq = ROPE_THETA ** (-np.arange(0, ROPE_AXIS_DIM, 2, dtype=np.float32) / ROPE_AXIS_DIM)
    ang_r = (pos // GRID_W).astype(np.float32)[:, None] * inv_freq.astype(np.float32)
    ang_c = (pos % GRID_W).astype(np.float32)[:, None] * inv_freq.astype(np.float32)
    ang_r = jnp.asarray(ang_r, F32)
    ang_c = jnp.asarray(ang_c, F32)
    cos = jnp.concatenate([jnp.cos(ang_r), jnp.cos(ang_r), jnp.cos(ang_c), jnp.cos(ang_c)], axis=1)
    sin = jnp.concatenate([-jnp.sin(ang_r), jnp.sin(ang_r), -jnp.sin(ang_c), jnp.sin(ang_c)], axis=1)
    cos = jnp.concatenate([cos, jnp.ones((ctx_len, HEAD_DIM), F32)], axis=0)
    sin = jnp.concatenate([sin, jnp.zeros((ctx_len, HEAD_DIM), F32)], axis=0)
    return cos, sin


def _attn_kernel(q_ref, k_ref, v_ref, o_ref):
    k = k_ref[...]
    v = v_ref[...]
    for g in range(Q_PER_KV):
        q = q_ref[:, g * HEAD_DIM:(g + 1) * HEAD_DIM]
        s = lax.dot_general(q, k, (((1,), (1,)), ((), ())), preferred_element_type=F32)
        m = jnp.max(s, axis=-1, keepdims=True)
        p = jnp.exp(s - m)
        l = jnp.sum(p, axis=-1, keepdims=True)
        o = jnp.dot(p.astype(BF16), v, preferred_element_type=F32)
        o_ref[:, g * HEAD_DIM:(g + 1) * HEAD_DIM] = (o / l).astype(BF16)


def _attention(q, k, v, seq, tq):
    bsz, lt, _ = q.shape
    gw = Q_PER_KV * HEAD_DIM
    return pl.pallas_call(
        _attn_kernel,
        grid=(bsz, N_KV_HEADS, seq // tq),
        in_specs=[
            pl.BlockSpec((None, tq, gw), lambda b, h, i: (b, i, h)),
            pl.BlockSpec((None, lt, HEAD_DIM), lambda b, h, i: (b, 0, h)),
            pl.BlockSpec((None, lt, HEAD_DIM), lambda b, h, i: (b, 0, h)),
        ],
        out_specs=pl.BlockSpec((None, tq, gw), lambda b, h, i: (b, i, h)),
        out_shape=jax.ShapeDtypeStruct((bsz, seq, ATTN_WIDTH), BF16),
        compiler_params=_params(("arbitrary", "arbitrary", "arbitrary"), 48),
        name="attention",
    )(q, k, v)


def _fourier_chan_kernel(f_ref, cc_ref, sc_ref, a_ref, nb_ref):
    for g in range(FOURIER_GROUPS):
        cols = slice(g * FOURIER_GROUP_DIM, (g + 1) * FOURIER_GROUP_DIM)
        u = f_ref[:, cols]
        a_ref[:, cols] = jnp.dot(u, cc_ref[...], preferred_element_type=F32).astype(BF16)
        nb_ref[:, cols] = (-jnp.dot(u, sc_ref[...], preferred_element_type=F32)).astype(BF16)


def _fourier_seq_kernel(cl_ref, sl_ref, a_ref, nb_ref, o_ref, acc_ref, *, scale):
    kk = pl.program_id(2)

    @pl.when(kk == 0)
    def _():
        acc_ref[...] = jnp.zeros_like(acc_ref)

    acc_ref[...] += (jnp.dot(cl_ref[...], a_ref[...], preferred_element_type=F32)
                     + jnp.dot(sl_ref[...], nb_ref[...], preferred_element_type=F32))

    @pl.when(kk == pl.num_programs(2) - 1)
    def _():
        o_ref[...] = (acc_ref[...] * scale).astype(BF16)


def _fourier_mix(f, seq):
    bsz = f.shape[0]
    gd = FOURIER_GROUP_DIM
    jk = np.outer(np.arange(gd), np.arange(gd)) % gd
    cc = jnp.asarray(np.cos(2 * np.pi * jk / gd), BF16)
    sc = jnp.asarray(np.sin(2 * np.pi * jk / gd), BF16)
    tm = _row_tile(seq, 512)
    a, nb = pl.pallas_call(
        _fourier_chan_kernel,
        grid=(bsz, seq // tm),
        in_specs=[
            pl.BlockSpec((None, tm, FOURIER_WIDTH), lambda b, i: (b, i, 0)),
            _resident((gd, gd), lambda b, i: (0, 0)),
            _resident((gd, gd), lambda b, i: (0, 0)),
        ],
        out_specs=[pl.BlockSpec((None, tm, FOURIER_WIDTH), lambda b, i: (b, i, 0))] * 2,
        out_shape=[jax.ShapeDtypeStruct((bsz, seq, FOURIER_WIDTH), BF16)] * 2,
        compiler_params=_params(("arbitrary", "arbitrary"), 32),
        name="fourier_channels",
    )(f, cc, sc)

    idx = lax.broadcasted_iota(jnp.int32, (seq, seq), 0) * lax.broadcasted_iota(jnp.int32, (seq, seq), 1)
    ang = (idx % seq).astype(F32) * np.float32(2 * np.pi / seq)
    cl = jnp.cos(ang).astype(BF16)
    sl = jnp.sin(ang).astype(BF16)
    to = _row_tile(seq, 1024)
    tk = _row_tile(seq, 512)
    return pl.pallas_call(
        functools.partial(_fourier_seq_kernel, scale=float((seq * gd) ** -0.5)),
        grid=(bsz, seq // to, seq // tk),
        in_specs=[
            pl.BlockSpec((to, tk), lambda b, i, k: (i, k)),
            pl.BlockSpec((to, tk), lambda b, i, k: (i, k)),
            pl.BlockSpec((None, tk, FOURIER_WIDTH), lambda b, i, k: (b, k, 0)),
            pl.BlockSpec((None, tk, FOURIER_WIDTH), lambda b, i, k: (b, k, 0)),
        ],
        out_specs=pl.BlockSpec((None, to, FOURIER_WIDTH), lambda b, i, k: (b, i, 0)),
        out_shape=jax.ShapeDtypeStruct((bsz, seq, FOURIER_WIDTH), BF16),
        scratch_shapes=[pltpu.VMEM((to, FOURIER_WIDTH), F32)],
        compiler_params=_params(("arbitrary", "arbitrary", "arbitrary"), 48),
        name="fourier_positions",
    )(cl, sl, a, nb)


def _ab_out_kernel(at_ref, fm_ref, wa_ref, wf_ref, x_ref, gate_ref, a2_ref, b2_ref, wr_ref, br_ref,
                   xo_ref, tok_ref, lg_ref):
    mix = (jnp.dot(at_ref[...], wa_ref[...], preferred_element_type=F32)
           + jnp.dot(fm_ref[...], wf_ref[...], preferred_element_type=F32))
    _residual_epilogue(mix, x_ref, gate_ref, a2_ref, b2_ref, wr_ref, br_ref, xo_ref, tok_ref, lg_ref)


def _ab_out(attn, fmix, w_attn, w_four, x, gate, a2, b2, w_router, b_router, seq, tm):
    n, d = x.shape
    n_exp = w_router.shape[1]
    ep_in, ep_out = _epilogue_specs(tm, d, n_exp, seq // tm)
    return pl.pallas_call(
        _ab_out_kernel,
        grid=(n // tm,),
        in_specs=[
            pl.BlockSpec((tm, ATTN_WIDTH), lambda i: (i, 0)),
            pl.BlockSpec((tm, FOURIER_WIDTH), lambda i: (i, 0)),
            _resident((ATTN_WIDTH, d), lambda i: (0, 0)),
            _resident((FOURIER_WIDTH, d), lambda i: (0, 0)),
        ] + ep_in,
        out_specs=ep_out,
        out_shape=_epilogue_out_shape(n, d, n_exp),
        compiler_params=_params(("arbitrary",), 48),
        name="ab_out_proj",
    )(attn, fmix, w_attn, w_four, x, gate, a2, b2, w_router, b_router)


def _cv_in_kernel(x_ref, a_ref, b_ref, w_ref, bias_ref, u_ref):
    d = x_ref.shape[1]
    h = _norm_mod(x_ref[...], a_ref[...], b_ref[...]).astype(BF16)
    y = jnp.dot(h, w_ref[...], preferred_element_type=F32) + bias_ref[...]
    u_ref[...] = y[:, :d] * jax.nn.sigmoid(y[:, d:])


def _cv_in(x, a_mod, b_mod, w_in, b_in, seq, tm):
    n, d = x.shape
    tiles = seq // tm
    bmap = lambda i: (i // tiles, 0, 0)
    return pl.pallas_call(
        _cv_in_kernel,
        grid=(n // tm,),
        in_specs=[
            pl.BlockSpec((tm, d), lambda i: (i, 0)),
            pl.BlockSpec((None, 1, d), bmap),
            pl.BlockSpec((None, 1, d), bmap),
            _resident((d, 2 * d), lambda i: (0, 0)),
            _resident((1, 2 * d), lambda i: (0, 0)),
        ],
        out_specs=pl.BlockSpec((tm, d), lambda i: (i, 0)),
        out_shape=jax.ShapeDtypeStruct((n, d), F32),
        compiler_params=_params(("arbitrary",), 48),
        name="conv_in_proj_glu",
    )(x, a_mod, b_mod, w_in, b_in)


def _dwconv_kernel(prev_ref, cur_ref, next_ref, w_ref, b_ref, o_ref, win_ref, *, window, row_chunk):
    i = pl.program_id(1)
    tm, tc = cur_ref.shape
    first = i == 0
    last = i == pl.num_programs(1) - 1
    win_ref[0:HALO, :] = jnp.where(first, 0.0, prev_ref[...])
    win_ref[HALO:HALO + tm, :] = cur_ref[...]
    win_ref[HALO + tm:, :] = jnp.where(last, 0.0, next_ref[...])
    pad = window // 2
    for c in range(tc // LANES):
        cols = slice(c * LANES, (c + 1) * LANES)
        for r in range(tm // row_chunk):
            acc = jnp.zeros((row_chunk, LANES), F32) + b_ref[:, cols]
            for j in range(window):
                start = HALO - pad + j + r * row_chunk
                acc = acc + w_ref[j:j + 1, cols] * win_ref[start:start + row_chunk, cols]
            o_ref[r * row_chunk:(r + 1) * row_chunk, cols] = acc


def _dwconv(u, w_dw, b_dw, tm, tc):
    bsz, seq, d = u.shape
    window = w_dw.shape[0]
    assert window // 2 <= HALO and tm % HALO == 0
    hb = tm // HALO
    n_halo = seq // HALO
    return pl.pallas_call(
        functools.partial(_dwconv_kernel, window=window, row_chunk=min(tm, 64)),
        grid=(bsz, seq // tm, d // tc),
        in_specs=[
            pl.BlockSpec((None, HALO, tc), lambda b, i, c: (b, jnp.maximum(i * hb - 1, 0), c)),
            pl.BlockSpec((None, tm, tc), lambda b, i, c: (b, i, c)),
            pl.BlockSpec((None, HALO, tc), lambda b, i, c: (b, jnp.minimum((i + 1) * hb, n_halo - 1), c)),
            pl.BlockSpec((window, tc), lambda b, i, c: (0, c)),
            pl.BlockSpec((1, tc), lambda b, i, c: (0, c)),
        ],
        out_specs=pl.BlockSpec((None, tm, tc), lambda b, i, c: (b, i, c)),
        out_shape=jax.ShapeDtypeStruct((bsz, seq, d), F32),
        scratch_shapes=[pltpu.VMEM((tm + 2 * HALO, tc), F32)],
        compiler_params=_params(("arbitrary", "arbitrary", "arbitrary"), 32),
        name="depthwise_conv",
    )(u, u, u, w_dw, b_dw)


def _cv_out_kernel(v_ref, lng_ref, lnb_ref, w_ref, bo_ref, x_ref, gate_ref, a2_ref, b2_ref, wr_ref, br_ref,
                   xo_ref, tok_ref, lg_ref):
    v = v_ref[...]
    mu = jnp.mean(v, axis=-1, keepdims=True)
    vc = v - mu
    var = jnp.mean(vc * vc, axis=-1, keepdims=True)
    y = vc * lax.rsqrt(var + EPS) * lng_ref[...] + lnb_ref[...]
    y = (y * jax.nn.sigmoid(y)).astype(BF16)
    mix = jnp.dot(y, w_ref[...], preferred_element_type=F32) + bo_ref[...]
    _residual_epilogue(mix, x_ref, gate_ref, a2_ref, b2_ref, wr_ref, br_ref, xo_ref, tok_ref, lg_ref)


def _cv_out(v, ln_g, ln_b, w_out, b_out, x, gate, a2, b2, w_router, b_router, seq, tm):
    n, d = x.shape
    n_exp = w_router.shape[1]
    ep_in, ep_out = _epilogue_specs(tm, d, n_exp, seq // tm)
    return pl.pallas_call(
        _cv_out_kernel,
        grid=(n // tm,),
        in_specs=[
            pl.BlockSpec((tm, d), lambda i: (i, 0)),
            _resident((1, d), lambda i: (0, 0)),
            _resident((1, d), lambda i: (0, 0)),
            _resident((d, d), lambda i: (0, 0)),
            _resident((1, d), lambda i: (0, 0)),
        ] + ep_in,
        out_specs=ep_out,
        out_shape=_epilogue_out_shape(n, d, n_exp),
        compiler_params=_params(("arbitrary",), 48),
        name="conv_out_proj",
    )(v, ln_g, ln_b, w_out, b_out, x, gate, a2, b2, w_router, b_router)


def _route_kernel(lg_ref, idx_ref, gate_ref, rank_ref, cnt_ref, run_ref):
    i = pl.program_id(0)

    @pl.when(i == 0)
    def _():
        run_ref[...] = jnp.zeros_like(run_ref)

    lg = lg_ref[...]
    tr, n_exp = lg.shape
    lane = lax.broadcasted_iota(jnp.int32, (tr, n_exp), 1).astype(F32)
    work = lg
    vals, idxs = [], []
    member = jnp.zeros((tr, n_exp), F32)
    for _ in range(TOP_K):
        m = jnp.max(work, axis=-1, keepdims=True)
        sel = jnp.min(jnp.where(work == m, lane, float(n_exp)), axis=-1, keepdims=True)
        hit = lane == sel
        vals.append(m)
        idxs.append(sel)
        member = jnp.where(hit, 1.0, member)
        work = jnp.where(hit, -jnp.inf, work)
    es = [jnp.exp(v - vals[0]) for v in vals]
    tot = es[0] + es[1] + es[2] + es[3]
    r_i = lax.broadcasted_iota(jnp.int32, (tr, tr), 0)
    c_i = lax.broadcasted_iota(jnp.int32, (tr, tr), 1)
    tri = (c_i < r_i).astype(BF16)
    before = jnp.dot(tri, member.astype(BF16), preferred_element_type=F32) + run_ref[...]
    for k in range(TOP_K):
        idx_ref[:, k:k + 1] = idxs[k].astype(jnp.int32)
        gate_ref[:, k:k + 1] = es[k] / tot
        rank_ref[:, k:k + 1] = jnp.sum(jnp.where(lane == idxs[k], before, 0.0),
                                       axis=-1, keepdims=True).astype(jnp.int32)
    run_ref[...] += jnp.sum(member, axis=0, keepdims=True)
    cnt_ref[...] = run_ref[...].astype(jnp.int32)


def _route(logits, tr):
    n, n_exp = logits.shape
    return pl.pallas_call(
        _route_kernel,
        grid=(n // tr,),
        in_specs=[pl.BlockSpec((tr, n_exp), lambda i: (i, 0))],
        out_specs=[
            pl.BlockSpec((tr, TOP_K), lambda i: (i, 0)),
            pl.BlockSpec((tr, TOP_K), lambda i: (i, 0)),
            pl.BlockSpec((tr, TOP_K), lambda i: (i, 0)),
            pl.BlockSpec((1, n_exp), lambda i: (0, 0)),
        ],
        out_shape=[
            jax.ShapeDtypeStruct((n, TOP_K), jnp.int32),
            jax.ShapeDtypeStruct((n, TOP_K), F32),
            jax.ShapeDtypeStruct((n, TOP_K), jnp.int32),
            jax.ShapeDtypeStruct((1, n_exp), jnp.int32),
        ],
        scratch_shapes=[pltpu.VMEM((1, n_exp), F32)],
        compiler_params=_params(("arbitrary",), 32),
        name="moe_route",
    )(logits)


def _dispatch_kernel(pad_lo_ref, pad_n_ref, tail_ref, dest_ref, tok_ref, xs_ref, zero_ref, sem, zsem, *, p, n_exp):
    i = pl.program_id(0)
    n_assign = dest_ref.shape[0]
    blk = zero_ref.shape[0]

    def row_copy(a):
        t = a // TOP_K
        return pltpu.make_async_copy(tok_ref.at[pl.ds(t * p, p), :],
                                     xs_ref.at[pl.ds(dest_ref[a] * p, p), :], sem)

    def zero_row(row):
        return pltpu.make_async_copy(zero_ref.at[pl.ds(0, p), :], xs_ref.at[pl.ds(row * p, p), :], zsem)

    def zero_block(b):
        return pltpu.make_async_copy(zero_ref, xs_ref.at[pl.ds(b * blk, blk), :], zsem)

    def start(a, c):
        row_copy(a).start()
        return c

    def wait(a, c):
        row_copy(a).wait()
        return c

    lax.fori_loop(0, n_assign, start, 0)

    @pl.when(i == 0)
    def _():
        zero_ref[...] = jnp.zeros_like(zero_ref)
        for e in range(n_exp):
            lo = pad_lo_ref[e]
            lax.fori_loop(0, pad_n_ref[e], lambda j, c: (zero_row(lo + j).start(), c)[1], 0)
        first = tail_ref[0]
        lax.fori_loop(0, tail_ref[1], lambda j, c: (zero_block(first + j).start(), c)[1], 0)
        for e in range(n_exp):
            lo = pad_lo_ref[e]
            lax.fori_loop(0, pad_n_ref[e], lambda j, c: (zero_row(lo + j).wait(), c)[1], 0)
        lax.fori_loop(0, tail_ref[1], lambda j, c: (zero_block(first + j).wait(), c)[1], 0)

    lax.fori_loop(0, n_assign, wait, 0)


def _dispatch(pad_lo, pad_n, tail, dest_flat, tok_slab, n_rows, d, tt, tme):
    p = d // LANES
    n = tok_slab.shape[0] // p
    n_exp = pad_lo.shape[0]
    return pl.pallas_call(
        functools.partial(_dispatch_kernel, p=p, n_exp=n_exp),
        grid_spec=pltpu.PrefetchScalarGridSpec(
            num_scalar_prefetch=3,
            grid=(n // tt,),
            in_specs=[
                pl.BlockSpec((tt * TOP_K,), lambda i, lo, pn, tl: (i,), memory_space=pltpu.SMEM),
                pl.BlockSpec((tt * p, LANES), lambda i, lo, pn, tl: (i, 0)),
            ],
            out_specs=pl.BlockSpec(memory_space=pl.ANY),
            scratch_shapes=[pltpu.VMEM((tme * p, LANES), F32), pltpu.SemaphoreType.DMA,
                            pltpu.SemaphoreType.DMA],
        ),
        out_shape=jax.ShapeDtypeStruct((n_rows * p, LANES), F32),
        compiler_params=_params(("arbitrary",), 32),
        name="moe_dispatch",
    )(pad_lo, pad_n, tail, dest_flat, tok_slab)


def _expert_kernel(be_ref, nu_ref, xs_ref, w1g_ref, w1l_ref, b1g_ref, b1l_ref, w2_ref, b2_ref, y_ref, *, p):
    i = pl.program_id(0)

    @pl.when(i < nu_ref[0])
    def _():
        tme = xs_ref.shape[0] // p
        x = _from_slab(xs_ref, tme, p).astype(BF16)
        hg = jnp.dot(x, w1g_ref[...], preferred_element_type=F32) + b1g_ref[...]
        hl = jnp.dot(x, w1l_ref[...], preferred_element_type=F32) + b1l_ref[...]
        g = jnp.minimum(hg, SWIGLU_LIMIT)
        lin = jnp.clip(hl, -SWIGLU_LIMIT, SWIGLU_LIMIT)
        act = (g * jax.nn.sigmoid(SWIGLU_ALPHA * g) * (lin + 1.0)).astype(BF16)
        y = jnp.dot(act, w2_ref[...], preferred_element_type=F32) + b2_ref[...]
        _to_slab(y_ref, y, tme, p)

    @pl.when(i >= nu_ref[0])
    def _():
        y_ref[...] = jnp.zeros_like(y_ref)


def _experts(block_e, n_used, xs, w1g, w1l, b1g, b1l, w2, b2, tme):
    n_exp, d, ff = w1g.shape
    p = d // LANES
    n_rows = xs.shape[0] // p
    rows_in = lambda i, be, nu: (jnp.minimum(i, nu[0] - 1), 0)
    wsel = lambda i, be, nu: (be[i], 0, 0)
    return pl.pallas_call(
        functools.partial(_expert_kernel, p=p),
        grid_spec=pltpu.PrefetchScalarGridSpec(
            num_scalar_prefetch=2,
            grid=(n_rows // tme,),
            in_specs=[
                pl.BlockSpec((tme * p, LANES), rows_in),
                pl.BlockSpec((None, d, ff), wsel),
                pl.BlockSpec((None, d, ff), wsel),
                pl.BlockSpec((None, 1, ff), wsel),
                pl.BlockSpec((None, 1, ff), wsel),
                pl.BlockSpec((None, ff, d), wsel),
                pl.BlockSpec((None, 1, d), wsel),
            ],
            out_specs=pl.BlockSpec((tme * p, LANES), lambda i, be, nu: (i, 0)),
        ),
        out_shape=jax.ShapeDtypeStruct((n_rows * p, LANES), F32),
        compiler_params=_params(("arbitrary",), 56),
        name="moe_experts",
    )(block_e, n_used, xs, w1g, w1l, b1g, b1l, w2, b2)


def _combine_kernel(dest_ref, y_ref, gates_ref, x_ref, gate2_ref, gfin_ref, o_ref, buf_ref, sem, *, p, final):
    tt = x_ref.shape[0]

    def row_copy(a):
        t = a // TOP_K
        k = a % TOP_K
        return pltpu.make_async_copy(y_ref.at[pl.ds(dest_ref[a] * p, p), :],
                                     buf_ref.at[k, pl.ds(t * p, p), :], sem)

    lax.fori_loop(0, tt * TOP_K, lambda a, c: (row_copy(a).start(), c)[1], 0)
    lax.fori_loop(0, tt * TOP_K, lambda a, c: (row_copy(a).wait(), c)[1], 0)
    gates = gates_ref[...]
    acc = gates[:, 0:1] * _from_slab(buf_ref.at[0], tt, p)
    for k in range(1, TOP_K):
        acc = acc + gates[:, k:k + 1] * _from_slab(buf_ref.at[k], tt, p)
    xn = x_ref[...] + gate2_ref[...] * acc
    if final:
        ms = jnp.mean(xn * xn, axis=-1, keepdims=True)
        xn = xn * lax.rsqrt(ms + EPS) * gfin_ref[...]
    o_ref[...] = xn


def _combine(dest_flat, y_slab, gates, x, gate2, g_final, seq, tt, final):
    n, d = x.shape
    p = d // LANES
    tiles = seq // tt
    return pl.pallas_call(
        functools.partial(_combine_kernel, p=p, final=final),
        grid=(n // tt,),
        in_specs=[
            pl.BlockSpec((tt * TOP_K,), lambda i: (i,), memory_space=pltpu.SMEM),
            pl.BlockSpec(memory_space=pl.ANY),
            pl.BlockSpec((tt, TOP_K), lambda i: (i, 0)),
            pl.BlockSpec((tt, d), lambda i: (i, 0)),
            pl.BlockSpec((None, 1, d), lambda i: (i // tiles, 0, 0)),
            pl.BlockSpec((1, d), lambda i: (0, 0)),
        ],
        out_specs=pl.BlockSpec((tt, d), lambda i: (i, 0)),
        out_shape=jax.ShapeDtypeStruct((n, d), F32),
        scratch_shapes=[pltpu.VMEM((TOP_K, tt * p, LANES), F32), pltpu.SemaphoreType.DMA],
        compiler_params=_params(("arbitrary",), 32),
        name="moe_combine",
    )(dest_flat, y_slab, gates, x, gate2, g_final)


def _moe(x, tok_slab, logits, gate2, w1, b1, w2, b2, g_final, seq, final):
    n, d = x.shape
    n_exp = w1.shape[0]
    ff = w2.shape[1]
    tme = 256
    top_idx, gates, rank, counts = _route(logits, _row_tile(n, 256))
    counts = counts[0]
    padded = (counts + tme - 1) // tme * tme
    pad_end = jnp.cumsum(padded)
    pad_start = pad_end - padded
    dest = (pad_start[top_idx] + rank).reshape(-1).astype(jnp.int32)
    n_rows = (n * TOP_K + tme - 1) // tme * tme + n_exp * tme
    n_blocks = n_rows // tme
    block_e = jnp.minimum(
        jnp.searchsorted(pad_end, jnp.arange(n_blocks, dtype=jnp.int32) * tme, side='right'),
        n_exp - 1).astype(jnp.int32)
    n_used = (pad_end[-1] // tme).astype(jnp.int32).reshape(1)
    pad_lo = (pad_start + counts).astype(jnp.int32)
    pad_n = (padded - counts).astype(jnp.int32)

    tail = jnp.concatenate([n_used, n_blocks - n_used]).astype(jnp.int32)
    xs = _dispatch(pad_lo, pad_n, tail, dest, tok_slab, n_rows, d, _row_tile(n, 256), tme)
    w1g = w1[:, :, 0::2].astype(BF16)
    w1l = w1[:, :, 1::2].astype(BF16)
    b1g = b1[:, 0::2].reshape(n_exp, 1, ff)
    b1l = b1[:, 1::2].reshape(n_exp, 1, ff)
    y = _experts(block_e, n_used, xs, w1g, w1l, b1g, b1l, w2.astype(BF16), b2.reshape(n_exp, 1, d), tme)
    return _combine(dest, y, gates, x, gate2, g_final.reshape(1, d), seq, _row_tile(seq, 128), final)


def kernel(x, c, ctx, c_ctx, w_mod, b_mod, g_mix, g_ffn, ab_w_in, ab_q_gain, ab_k_gain, ab_w_out,
           cv_w_in, cv_b_in, cv_w_dw, cv_b_dw, cv_ln_g, cv_ln_b, cv_w_out, cv_b_out,
           moe_w_router, moe_b_router, moe_w1, moe_b1, moe_w2, moe_b2, g_final):
    bsz, seq, d = x.shape
    ctx_len = ctx.shape[1]
    depth = w_mod.shape[0]
    n = bsz * seq
    assert bsz + 1 <= MOD_ROWS and d % LANES == 0
    tm = _row_tile(seq, 256)
    assert ctx_len % tm == 0

    cond = jnp.zeros((MOD_ROWS, d), F32).at[:bsz].set(c).at[bsz].set(c_ctx)
    mod = _modulation(cond, w_mod, b_mod).reshape(depth, MOD_ROWS, N_MOD, d)

    xf = x.reshape(n, d)
    for i in range(depth):
        j = i // 2
        sh1, sc1, gt1, sh2, sc2, gt2 = (mod[i, :, m].reshape(MOD_ROWS, 1, d) for m in range(N_MOD))
        a1 = g_mix[i] * (1.0 + sc1)
        a2 = g_ffn[i] * (1.0 + sc2)
        w_router = moe_w_router[i]
        b_router = moe_b_router[i].reshape(1, -1)
        if i % 2 == 0:
            assert not any(k % 2 == 0 for k in range(i + 1, depth)), "context update path not needed at this depth"
            xin = jnp.concatenate([xf.reshape(bsz, seq, d), ctx], axis=1)
            cos_t, sin_t = _rope_tables(seq, ctx_len)
            q, k, v, f = _ab_in(xin, a1, sh1, ab_w_in[j].astype(BF16), ab_q_gain[j].reshape(1, -1),
                                ab_k_gain[j].reshape(1, -1), cos_t, sin_t, seq, tm)
            attn = _attention(q, k, v, seq, tm)
            fmix = _fourier_mix(f, seq)
            w_out = ab_w_out[j].astype(BF16)
            xf, tok, logits = _ab_out(attn.reshape(n, ATTN_WIDTH), fmix.reshape(n, FOURIER_WIDTH),
                                      w_out[:ATTN_WIDTH], w_out[ATTN_WIDTH:], xf, gt1, a2, sh2,
                                      w_router, b_router, seq, tm)
        else:
            u = _cv_in(xf, a1, sh1, cv_w_in[j].astype(BF16), cv_b_in[j].reshape(1, -1), seq, tm)
            v = _dwconv(u.reshape(bsz, seq, d), cv_w_dw[j], cv_b_dw[j].reshape(1, -1),
                        _row_tile(seq, 128), _row_tile(d, 512))
            xf, tok, logits = _cv_out(v.reshape(n, d), cv_ln_g[j].reshape(1, -1), cv_ln_b[j].reshape(1, -1),
                                      cv_w_out[j].astype(BF16), cv_b_out[j].reshape(1, -1), xf, gt1, a2, sh2,
                                      w_router, b_router, seq, tm)
        xf = _moe(xf, tok, logits, gt2, moe_w1[i], moe_b1[i], moe_w2[i], moe_b2[i], g_final, seq,
                  final=(i == depth - 1))
    return xf.reshape(bsz, seq, d)
```

```python
import functools

import numpy as np
import jax
import jax.numpy as jnp
from jax import lax
from jax.experimental import pallas as pl
from jax.experimental.pallas import tpu as pltpu

F32 = jnp.float32
BF16 = jnp.bfloat16
HIGHEST = lax.Precision.HIGHEST

LANES = 128
SUBLANES = 8
HEAD_DIM = 128
N_Q_HEADS = 8
N_KV_HEADS = 2
Q_PER_KV = N_Q_HEADS // N_KV_HEADS
ATTN_WIDTH = N_Q_HEADS * HEAD_DIM
KV_WIDTH = N_KV_HEADS * HEAD_DIM
GRID_W = 64
ROPE_AXIS_DIM = HEAD_DIM // 2
ROPE_THETA = 10000.0
FOURIER_GROUPS = 8
FOURIER_GROUP_DIM = 128
FOURIER_WIDTH = FOURIER_GROUPS * FOURIER_GROUP_DIM
TOP_K = 4
SWIGLU_LIMIT = 7.0
SWIGLU_ALPHA = 1.702
N_MOD = 6
EPS = 1e-6
MOD_ROWS = 8
HALO = 16
MIB = 1024 * 1024


def _params(sem, vmem_mib):
    return pltpu.CompilerParams(dimension_semantics=sem, vmem_limit_bytes=vmem_mib * MIB)


def _row_tile(n, want):
    t = min(n, want)
    assert n % t == 0, (n, t)
    return t


def _resident(shape, index_map):
    return pl.BlockSpec(shape, index_map, pipeline_mode=pl.Buffered(1))


def _mod_kernel(c_ref, w_ref, b_ref, o_ref):
    c = c_ref[...]
    s = c * jax.nn.sigmoid(c)
    o_ref[...] = jnp.dot(s, w_ref[...], preferred_element_type=F32, precision=HIGHEST) + b_ref[...]


def _modulation(cond, w_mod, b_mod):
    depth, d, n = w_mod.shape
    tn = _row_tile(n, min(1024, d))
    return pl.pallas_call(
        _mod_kernel,
        grid=(depth, n // tn),
        in_specs=[
            pl.BlockSpec((MOD_ROWS, d), lambda i, j: (0, 0)),
            pl.BlockSpec((None, d, tn), lambda i, j: (i, 0, j)),
            pl.BlockSpec((None, 1, tn), lambda i, j: (i, 0, j)),
        ],
        out_specs=pl.BlockSpec((None, MOD_ROWS, tn), lambda i, j: (i, 0, j)),
        out_shape=jax.ShapeDtypeStruct((depth, MOD_ROWS, n), F32),
        compiler_params=_params(("arbitrary", "arbitrary"), 40),
        name="adaln_modulation",
    )(cond, w_mod, b_mod.reshape(depth, 1, n))


def _norm_mod(x, a, b):
    ms = jnp.mean(x * x, axis=-1, keepdims=True)
    return x * lax.rsqrt(ms + EPS) * a + b


def _to_slab(ref, val, rows, p):
    for j in range(p):
        ref[pl.ds(j, rows, stride=p), :] = val[:, j * LANES:(j + 1) * LANES]


def _from_slab(ref, rows, p):
    return jnp.concatenate([ref[pl.ds(j, rows, stride=p), :] for j in range(p)], axis=1)


def _split_bf16(v):
    hi = v.astype(BF16)
    return hi, (v - hi.astype(F32)).astype(BF16)


def _residual_epilogue(mix, x_ref, gate_ref, a2_ref, b2_ref, wrh_ref, wrl_ref, br_ref, xo_ref, tok_ref, lg_ref):
    tm, d = x_ref.shape
    xn = x_ref[...] + gate_ref[...] * mix
    xo_ref[...] = xn
    tok = _norm_mod(xn, a2_ref[...], b2_ref[...])
    tok_hi, tok_lo = _split_bf16(tok)
    nt = (((1,), (1,)), ((), ()))
    wrh = wrh_ref[...]
    lg = (lax.dot_general(wrh, tok_hi, nt, preferred_element_type=F32)
          + lax.dot_general(wrh, tok_lo, nt, preferred_element_type=F32)
          + lax.dot_general(wrl_ref[...], tok_hi, nt, preferred_element_type=F32))
    lg_ref[...] = lg + br_ref[...]
    _to_slab(tok_ref, tok, tm, d // LANES)


def _epilogue_specs(tm, d, n_exp, tiles_per_batch):
    bmap = lambda i: (i // tiles_per_batch, 0, 0)
    in_specs = [
        pl.BlockSpec((tm, d), lambda i: (i, 0)),
        pl.BlockSpec((None, 1, d), bmap),
        pl.BlockSpec((None, 1, d), bmap),
        pl.BlockSpec((None, 1, d), bmap),
        _resident((n_exp, d), lambda i: (0, 0)),
        _resident((n_exp, d), lambda i: (0, 0)),
        _resident((n_exp, 1), lambda i: (0, 0)),
    ]
    p = d // LANES
    out_specs = [
        pl.BlockSpec((tm, d), lambda i: (i, 0)),
        pl.BlockSpec((tm * p, LANES), lambda i: (i, 0)),
        pl.BlockSpec((n_exp, tm), lambda i: (0, i)),
    ]
    return in_specs, out_specs


def _epilogue_out_shape(n, d, n_exp):
    return [
        jax.ShapeDtypeStruct((n, d), F32),
        jax.ShapeDtypeStruct((n * (d // LANES), LANES), F32),
        jax.ShapeDtypeStruct((n_exp, n), F32),
    ]


def _router_operands(w_router, b_router):
    wt = w_router.T
    hi = wt.astype(BF16)
    lo = (wt - hi.astype(F32)).astype(BF16)
    return hi, lo, b_router.reshape(-1, 1)


def _ab_in_kernel(x_ref, a_ref, b_ref, w_ref, qg_ref, kg_ref, cos_ref, sin_ref,
                  q_ref, k_ref, v_ref, f_ref):
    h = _norm_mod(x_ref[...], a_ref[...], b_ref[...]).astype(BF16)
    y = jnp.dot(h, w_ref[...], preferred_element_type=F32)
    cos = cos_ref[...]
    sin = sin_ref[...]
    lane = lax.broadcasted_iota(jnp.int32, cos.shape, 1)
    low = (lane % (2 * (ROPE_AXIS_DIM // 2))) < (ROPE_AXIS_DIM // 2)

    def head(col, gain, scale):
        blk = y[:, col:col + HEAD_DIM]
        ms = jnp.mean(blk * blk, axis=-1, keepdims=True)
        r = blk * lax.rsqrt(ms + EPS) * gain
        up = pltpu.roll(r, HEAD_DIM - ROPE_AXIS_DIM // 2, axis=1)
        dn = pltpu.roll(r, ROPE_AXIS_DIM // 2, axis=1)
        return (r * cos + jnp.where(low, up, dn) * sin) * scale

    for hd in range(N_Q_HEADS):
        q_ref[:, hd * HEAD_DIM:(hd + 1) * HEAD_DIM] = head(
            hd * HEAD_DIM, qg_ref[...], HEAD_DIM ** -0.5).astype(BF16)
    for hd in range(N_KV_HEADS):
        k_ref[:, hd * HEAD_DIM:(hd + 1) * HEAD_DIM] = head(
            ATTN_WIDTH + hd * HEAD_DIM, kg_ref[...], 1.0).astype(BF16)
    v_ref[...] = y[:, ATTN_WIDTH + KV_WIDTH:ATTN_WIDTH + 2 * KV_WIDTH].astype(BF16)
    f_ref[...] = y[:, ATTN_WIDTH + 2 * KV_WIDTH:].astype(BF16)


def _ab_in(xin, a_mod, b_mod, w_in, q_gain, k_gain, cos_t, sin_t, seq, tm):
    bsz, lt, d = xin.shape
    n_out = w_in.shape[1]
    lat_tiles = seq // tm
    mmap = lambda b, t: (jnp.where(t < lat_tiles, b, bsz), 0, 0)
    row = lambda b, t: (b, t, 0)
    return pl.pallas_call(
        _ab_in_kernel,
        grid=(bsz, lt // tm),
        in_specs=[
            pl.BlockSpec((None, tm, d), row),
            pl.BlockSpec((None, 1, d), mmap),
            pl.BlockSpec((None, 1, d), mmap),
            _resident((d, n_out), lambda b, t: (0, 0)),
            _resident((1, HEAD_DIM), lambda b, t: (0, 0)),
            _resident((1, HEAD_DIM), lambda b, t: (0, 0)),
            pl.BlockSpec((tm, HEAD_DIM), lambda b, t: (t, 0)),
            pl.BlockSpec((tm, HEAD_DIM), lambda b, t: (t, 0)),
        ],
        out_specs=[
            pl.BlockSpec((None, tm, ATTN_WIDTH), row),
            pl.BlockSpec((None, tm, KV_WIDTH), row),
            pl.BlockSpec((None, tm, KV_WIDTH), row),
            pl.BlockSpec((None, tm, FOURIER_WIDTH), row),
        ],
        out_shape=[
            jax.ShapeDtypeStruct((bsz, lt, ATTN_WIDTH), BF16),
            jax.ShapeDtypeStruct((bsz, lt, KV_WIDTH), BF16),
            jax.ShapeDtypeStruct((bsz, lt, KV_WIDTH), BF16),
            jax.ShapeDtypeStruct((bsz, lt, FOURIER_WIDTH), BF16),
        ],
        compiler_params=_params(("arbitrary", "arbitrary"), 48),
        name="ab_in_proj",
    )(xin, a_mod, b_mod, w_in, q_gain, k_gain, cos_t, sin_t)


def _rope_tables(seq, ctx_len):
    pos = np.arange(seq)
    inv_freq = ROPE_THETA ** (-np.arange(0, ROPE_AXIS_DIM, 2, dtype=np.float32) / ROPE_AXIS_DIM)
    ang_r = (pos // GRID_W).astype(np.float32)[:, None] * inv_freq.astype(np.float32)
    ang_c = (pos % GRID_W).astype(np.float32)[:, None] * inv_freq.astype(np.float32)
    ang_r = jnp.asarray(ang_r, F32)
    ang_c = jnp.asarray(ang_c, F32)
    cos = jnp.concatenate([jnp.cos(ang_r), jnp.cos(ang_r), jnp.cos(ang_c), jnp.cos(ang_c)], axis=1)
    sin = jnp.concatenate([-jnp.sin(ang_r), jnp.sin(ang_r), -jnp.sin(ang_c), jnp.sin(ang_c)], axis=1)
    cos = jnp.concatenate([cos, jnp.ones((ctx_len, HEAD_DIM), F32)], axis=0)
    sin = jnp.concatenate([sin, jnp.zeros((ctx_len, HEAD_DIM), F32)], axis=0)
    return cos, sin


def _attn_kernel(q_ref, k_ref, v_ref, o_ref):
    k = k_ref[...]
    v = v_ref[...]
    for g in range(Q_PER_KV):
        q = q_ref[:, g * HEAD_DIM:(g + 1) * HEAD_DIM]
        s = lax.dot_general(q, k, (((1,), (1,)), ((), ())), preferred_element_type=F32)
        m = jnp.max(s, axis=-1, keepdims=True)
        p = jnp.exp(s - m)
        l = jnp.sum(p, axis=-1, keepdims=True)
        o = jnp.dot(p.astype(BF16), v, preferred_element_type=F32)
        o_ref[:, g * HEAD_DIM:(g + 1) * HEAD_DIM] = (o / l).astype(BF16)


def _attention(q, k, v, seq, tq):
    bsz, lt, _ = q.shape
    gw = Q_PER_KV * HEAD_DIM
    return pl.pallas_call(
        _attn_kernel,
        grid=(bsz, N_KV_HEADS, seq // tq),
        in_specs=[
            pl.BlockSpec((None, tq, gw), lambda b, h, i: (b, i, h)),
            pl.BlockSpec((None, lt, HEAD_DIM), lambda b, h, i: (b, 0, h)),
            pl.BlockSpec((None, lt, HEAD_DIM), lambda b, h, i: (b, 0, h)),
        ],
        out_specs=pl.BlockSpec((None, tq, gw), lambda b, h, i: (b, i, h)),
        out_shape=jax.ShapeDtypeStruct((bsz, seq, ATTN_WIDTH), BF16),
        compiler_params=_params(("arbitrary", "arbitrary", "arbitrary"), 48),
        name="attention",
    )(q, k, v)


def _fourier_chan_kernel(f_ref, cr_ref, ci_ref, zr_ref, zi_ref):
    for g in range(FOURIER_GROUPS):
        cols = slice(g * FOURIER_GROUP_DIM, (g + 1) * FOURIER_GROUP_DIM)
        u = f_ref[:, cols]
        zr_ref[:, cols] = jnp.dot(u, cr_ref[...], preferred_element_type=F32)
        zi_ref[:, cols] = jnp.dot(u, ci_ref[...], preferred_element_type=F32)


def _fourier_outer_kernel(zr_ref, zi_ref, fr_ref, fi_ref, yr_ref, yi_ref):
    fr = fr_ref[...]
    fi = fi_ref[...]
    for jj in range(zr_ref.shape[1]):
        zr = zr_ref[:, jj, :].astype(BF16)
        zi = zi_ref[:, jj, :].astype(BF16)
        yr_ref[:, jj, :] = (jnp.dot(fr, zr, preferred_element_type=F32)
                            - jnp.dot(fi, zi, preferred_element_type=F32))
        yi_ref[:, jj, :] = (jnp.dot(fr, zi, preferred_element_type=F32)
                            + jnp.dot(fi, zr, preferred_element_type=F32))


def _fourier_inner_kernel(yr_ref, yi_ref, twr_ref, twi_ref, gr_ref, gi_ref, o_ref, *, scale):
    gr = gr_ref[...]
    gi = gi_ref[...]
    for kk in range(yr_ref.shape[0]):
        yr = yr_ref[kk]
        yi = yi_ref[kk]
        twr = twr_ref[kk]
        twi = twi_ref[kk]
        tr = (twr * yr - twi * yi).astype(BF16)
        ti = (twr * yi + twi * yr).astype(BF16)
        x = jnp.dot(gr, tr, preferred_element_type=F32) - jnp.dot(gi, ti, preferred_element_type=F32)
        o_ref[:, kk, :] = x * scale


def _dft_parts(n):
    jk = np.outer(np.arange(n), np.arange(n)) % n
    ang = 2 * np.pi * jk / n
    return np.cos(ang), -np.sin(ang)


def _fourier_mix(f, seq):
    bsz = f.shape[0]
    gd = FOURIER_GROUP_DIM
    width = FOURIER_WIDTH
    cr, ci = _dft_parts(gd)
    tm = _row_tile(seq, 512)
    zr, zi = pl.pallas_call(
        _fourier_chan_kernel,
        grid=(bsz, seq // tm),
        in_specs=[
            pl.BlockSpec((None, tm, width), lambda b, i: (b, i, 0)),
            _resident((gd, gd), lambda b, i: (0, 0)),
            _resident((gd, gd), lambda b, i: (0, 0)),
        ],
        out_specs=[pl.BlockSpec((None, tm, width), lambda b, i: (b, i, 0))] * 2,
        out_shape=[jax.ShapeDtypeStruct((bsz, seq, width), F32)] * 2,
        compiler_params=_params(("arbitrary", "arbitrary"), 32),
        name="fourier_channels",
    )(f, jnp.asarray(cr, BF16), jnp.asarray(ci, BF16))

    n_b = min(seq, 64)
    n_a = seq // n_b
    g2 = min(n_b, SUBLANES)
    fr, fi = _dft_parts(n_a)
    split = lambda b, j: (b, 0, j, 0)
    yr, yi = pl.pallas_call(
        _fourier_outer_kernel,
        grid=(bsz, n_b // g2),
        in_specs=[
            pl.BlockSpec((None, n_a, g2, width), split),
            pl.BlockSpec((None, n_a, g2, width), split),
            _resident((n_a, n_a), lambda b, j: (0, 0)),
            _resident((n_a, n_a), lambda b, j: (0, 0)),
        ],
        out_specs=[pl.BlockSpec((None, n_a, g2, width), split)] * 2,
        out_shape=[jax.ShapeDtypeStruct((bsz, n_a, n_b, width), F32)] * 2,
        compiler_params=_params(("arbitrary", "arbitrary"), 40),
        name="fourier_outer_dft",
    )(zr.reshape(bsz, n_a, n_b, width), zi.reshape(bsz, n_a, n_b, width),
      jnp.asarray(fr, BF16), jnp.asarray(fi, BF16))

    ang = 2 * np.pi * np.outer(np.arange(n_a), np.arange(n_b)) / seq
    twr = jnp.asarray(np.cos(ang)[:, :, None], F32)
    twi = jnp.asarray(-np.sin(ang)[:, :, None], F32)
    gr, gi = _dft_parts(n_b)
    g1 = min(n_a, 16)
    out = pl.pallas_call(
        functools.partial(_fourier_inner_kernel, scale=float((seq * gd) ** -0.5)),
        grid=(bsz, n_a // g1),
        in_specs=[
            pl.BlockSpec((None, g1, n_b, width), lambda b, j: (b, j, 0, 0)),
            pl.BlockSpec((None, g1, n_b, width), lambda b, j: (b, j, 0, 0)),
            pl.BlockSpec((g1, n_b, 1), lambda b, j: (j, 0, 0)),
            pl.BlockSpec((g1, n_b, 1), lambda b, j: (j, 0, 0)),
            _resident((n_b, n_b), lambda b, j: (0, 0)),
            _resident((n_b, n_b), lambda b, j: (0, 0)),
        ],
        out_specs=pl.BlockSpec((None, n_b, g1, width), lambda b, j: (b, 0, j, 0)),
        out_shape=jax.ShapeDtypeStruct((bsz, n_b, n_a, width), F32),
        compiler_params=_params(("arbitrary", "arbitrary"), 48),
        name="fourier_inner_dft",
    )(yr, yi, twr, twi, jnp.asarray(gr, BF16), jnp.asarray(gi, BF16))
    return out.reshape(bsz, seq, width)


def _ab_out_kernel(at_ref, fm_ref, wa_ref, wf_ref, x_ref, gate_ref, a2_ref, b2_ref, wrh_ref, wrl_ref, br_ref,
                   xo_ref, tok_ref, lg_ref):
    mix = (jnp.dot(at_ref[...], wa_ref[...], preferred_element_type=F32)
           + jnp.dot(fm_ref[...].astype(BF16), wf_ref[...], preferred_element_type=F32))
    _residual_epilogue(mix, x_ref, gate_ref, a2_ref, b2_ref, wrh_ref, wrl_ref, br_ref, xo_ref, tok_ref, lg_ref)


def _ab_out(attn, fmix, w_attn, w_four, x, gate, a2, b2, router, seq, tm):
    n, d = x.shape
    n_exp = router[0].shape[0]
    ep_in, ep_out = _epilogue_specs(tm, d, n_exp, seq // tm)
    return pl.pallas_call(
        _ab_out_kernel,
        grid=(n // tm,),
        in_specs=[
            pl.BlockSpec((tm, ATTN_WIDTH), lambda i: (i, 0)),
            pl.BlockSpec((tm, FOURIER_WIDTH), lambda i: (i, 0)),
            _resident((ATTN_WIDTH, d), lambda i: (0, 0)),
            _resident((FOURIER_WIDTH, d), lambda i: (0, 0)),
        ] + ep_in,
        out_specs=ep_out,
        out_shape=_epilogue_out_shape(n, d, n_exp),
        compiler_params=_params(("arbitrary",), 48),
        name="ab_out_proj",
    )(attn, fmix, w_attn, w_four, x, gate, a2, b2, *router)


def _cv_in_kernel(x_ref, a_ref, b_ref, w_ref, bias_ref, u_ref):
    d = x_ref.shape[1]
    h = _norm_mod(x_ref[...], a_ref[...], b_ref[...]).astype(BF16)
    y = jnp.dot(h, w_ref[...], preferred_element_type=F32) + bias_ref[...]
    u_ref[...] = y[:, :d] * jax.nn.sigmoid(y[:, d:])


def _cv_in(x, a_mod, b_mod, w_in, b_in, seq, tm):
    n, d = x.shape
    tiles = seq // tm
    bmap = lambda i: (i // tiles, 0, 0)
    return pl.pallas_call(
        _cv_in_kernel,
        grid=(n // tm,),
        in_specs=[
            pl.BlockSpec((tm, d), lambda i: (i, 0)),
            pl.BlockSpec((None, 1, d), bmap),
            pl.BlockSpec((None, 1, d), bmap),
            _resident((d, 2 * d), lambda i: (0, 0)),
            _resident((1, 2 * d), lambda i: (0, 0)),
        ],
        out_specs=pl.BlockSpec((tm, d), lambda i: (i, 0)),
        out_shape=jax.ShapeDtypeStruct((n, d), F32),
        compiler_params=_params(("arbitrary",), 48),
        name="conv_in_proj_glu",
    )(x, a_mod, b_mod, w_in, b_in)


def _dwconv_kernel(prev_ref, cur_ref, next_ref, w_ref, b_ref, o_ref, win_ref, *, window, row_chunk):
    i = pl.program_id(1)
    tm, tc = cur_ref.shape
    first = i == 0
    last = i == pl.num_programs(1) - 1
    win_ref[0:HALO, :] = jnp.where(first, 0.0, prev_ref[...])
    win_ref[HALO:HALO + tm, :] = cur_ref[...]
    win_ref[HALO + tm:, :] = jnp.where(last, 0.0, next_ref[...])
    pad = window // 2
    for c in range(tc // LANES):
        cols = slice(c * LANES, (c + 1) * LANES)
        for r in range(tm // row_chunk):
            acc = jnp.zeros((row_chunk, LANES), F32) + b_ref[:, cols]
            for j in range(window):
                start = HALO - pad + j + r * row_chunk
                acc = acc + w_ref[j:j + 1, cols] * win_ref[start:start + row_chunk, cols]
            o_ref[r * row_chunk:(r + 1) * row_chunk, cols] = acc


def _dwconv(u, w_dw, b_dw, tm, tc):
    bsz, seq, d = u.shape
    window = w_dw.shape[0]
    assert window // 2 <= HALO and tm % HALO == 0
    hb = tm // HALO
    n_halo = seq // HALO
    return pl.pallas_call(
        functools.partial(_dwconv_kernel, window=window, row_chunk=min(tm, 64)),
        grid=(bsz, seq // tm, d // tc),
        in_specs=[
            pl.BlockSpec((None, HALO, tc), lambda b, i, c: (b, jnp.maximum(i * hb - 1, 0), c)),
            pl.BlockSpec((None, tm, tc), lambda b, i, c: (b, i, c)),
            pl.BlockSpec((None, HALO, tc), lambda b, i, c: (b, jnp.minimum((i + 1) * hb, n_halo - 1), c)),
            pl.BlockSpec((window, tc), lambda b, i, c: (0, c)),
            pl.BlockSpec((1, tc), lambda b, i, c: (0, c)),
        ],
        out_specs=pl.BlockSpec((None, tm, tc), lambda b, i, c: (b, i, c)),
        out_shape=jax.ShapeDtypeStruct((bsz, seq, d), F32),
        scratch_shapes=[pltpu.VMEM((tm + 2 * HALO, tc), F32)],
        compiler_params=_params(("arbitrary", "arbitrary", "arbitrary"), 32),
        name="depthwise_conv",
    )(u, u, u, w_dw, b_dw)


def _cv_out_kernel(v_ref, lng_ref, lnb_ref, w_ref, bo_ref, x_ref, gate_ref, a2_ref, b2_ref, wrh_ref, wrl_ref, br_ref,
                   xo_ref, tok_ref, lg_ref):
    v = v_ref[...]
    mu = jnp.mean(v, axis=-1, keepdims=True)
    vc = v - mu
    var = jnp.mean(vc * vc, axis=-1, keepdims=True)
    y = vc * lax.rsqrt(var + EPS) * lng_ref[...] + lnb_ref[...]
    y = (y * jax.nn.sigmoid(y)).astype(BF16)
    mix = jnp.dot(y, w_ref[...], preferred_element_type=F32) + bo_ref[...]
    _residual_epilogue(mix, x_ref, gate_ref, a2_ref, b2_ref, wrh_ref, wrl_ref, br_ref, xo_ref, tok_ref, lg_ref)


def _cv_out(v, ln_g, ln_b, w_out, b_out, x, gate, a2, b2, router, seq, tm):
    n, d = x.shape
    n_exp = router[0].shape[0]
    ep_in, ep_out = _epilogue_specs(tm, d, n_exp, seq // tm)
    return pl.pallas_call(
        _cv_out_kernel,
        grid=(n // tm,),
        in_specs=[
            pl.BlockSpec((tm, d), lambda i: (i, 0)),
            _resident((1, d), lambda i: (0, 0)),
            _resident((1, d), lambda i: (0, 0)),
            _resident((d, d), lambda i: (0, 0)),
            _resident((1, d), lambda i: (0, 0)),
        ] + ep_in,
        out_specs=ep_out,
        out_shape=_epilogue_out_shape(n, d, n_exp),
        compiler_params=_params(("arbitrary",), 48),
        name="conv_out_proj",
    )(v, ln_g, ln_b, w_out, b_out, x, gate, a2, b2, *router)


def _route_kernel(lg_ref, idx_ref, gate_ref, rank_ref, cnt_ref, run_ref):
    i = pl.program_id(0)

    @pl.when(i == 0)
    def _():
        run_ref[...] = jnp.zeros_like(run_ref)

    lg = lg_ref[...]
    n_exp, tr = lg.shape
    sub = lax.broadcasted_iota(jnp.int32, (n_exp, tr), 0).astype(F32)
    work = lg
    vals, idxs = [], []
    member = jnp.zeros((n_exp, tr), F32)
    for _ in range(TOP_K):
        m = jnp.max(work, axis=0, keepdims=True)
        sel = jnp.min(jnp.where(work == m, sub, float(n_exp)), axis=0, keepdims=True)
        hit = sub == sel
        vals.append(m)
        idxs.append(sel)
        member = jnp.where(hit, 1.0, member)
        work = jnp.where(hit, -jnp.inf, work)
    es = [jnp.exp(v - vals[0]) for v in vals]
    tot = es[0] + es[1] + es[2] + es[3]
    r_i = lax.broadcasted_iota(jnp.int32, (tr, tr), 0)
    c_i = lax.broadcasted_iota(jnp.int32, (tr, tr), 1)
    earlier = (r_i < c_i).astype(BF16)
    before = jnp.dot(member.astype(BF16), earlier, preferred_element_type=F32) + run_ref[...]
    for k in range(TOP_K):
        idx_ref[k:k + 1, :] = idxs[k].astype(jnp.int32)
        gate_ref[k:k + 1, :] = es[k] / tot
        rank_ref[k:k + 1, :] = jnp.sum(jnp.where(sub == idxs[k], before, 0.0),
                                       axis=0, keepdims=True).astype(jnp.int32)
    run_ref[...] += jnp.sum(member, axis=1, keepdims=True)
    cnt_ref[...] = run_ref[...].astype(jnp.int32)


def _route(logits_t, tr):
    n_exp, n = logits_t.shape
    tok = lambda i: (0, i)
    return pl.pallas_call(
        _route_kernel,
        grid=(n // tr,),
        in_specs=[pl.BlockSpec((n_exp, tr), tok)],
        out_specs=[
            pl.BlockSpec((TOP_K, tr), tok),
            pl.BlockSpec((TOP_K, tr), tok),
            pl.BlockSpec((TOP_K, tr), tok),
            pl.BlockSpec((n_exp, 1), lambda i: (0, 0)),
        ],
        out_shape=[
            jax.ShapeDtypeStruct((TOP_K, n), jnp.int32),
            jax.ShapeDtypeStruct((TOP_K, n), F32),
            jax.ShapeDtypeStruct((TOP_K, n), jnp.int32),
            jax.ShapeDtypeStruct((n_exp, 1), jnp.int32),
        ],
        scratch_shapes=[pltpu.VMEM((n_exp, 1), F32)],
        compiler_params=_params(("arbitrary",), 32),
        name="moe_route",
    )(logits_t)


def _w1_split_kernel(w_ref, perm_ref, g_ref, l_ref):
    wide = 2 * LANES
    for c in range(w_ref.shape[1] // wide):
        t = jnp.dot(w_ref[:, c * wide:(c + 1) * wide].astype(BF16), perm_ref[...], preferred_element_type=F32)
        g_ref[:, c * LANES:(c + 1) * LANES] = t[:, :LANES].astype(BF16)
        l_ref[:, c * LANES:(c + 1) * LANES] = t[:, LANES:].astype(BF16)


def _w1_split(w1):
    n_exp, d, ff2 = w1.shape
    ff = ff2 // 2
    assert ff % LANES == 0
    perm = np.zeros((2 * LANES, 2 * LANES), np.float32)
    perm[2 * np.arange(LANES), np.arange(LANES)] = 1.0
    perm[2 * np.arange(LANES) + 1, LANES + np.arange(LANES)] = 1.0
    tk = _row_tile(d, 512)
    return pl.pallas_call(
        _w1_split_kernel,
        grid=(n_exp, d // tk),
        in_specs=[
            pl.BlockSpec((None, tk, ff2), lambda e, r: (e, r, 0)),
            _resident((2 * LANES, 2 * LANES), lambda e, r: (0, 0)),
        ],
        out_specs=[pl.BlockSpec((None, tk, ff), lambda e, r: (e, r, 0))] * 2,
        out_shape=[jax.ShapeDtypeStruct((n_exp, d, ff), BF16)] * 2,
        compiler_params=_params(("arbitrary", "arbitrary"), 32),
        name="moe_w1_split",
    )(w1, jnp.asarray(perm, BF16))


def _dispatch_kernel(pad_lo_ref, pad_n_ref, tail_ref, dest_ref, tok_ref, xs_ref, zero_ref, sem, zsem, *, p, n_exp):
    i = pl.program_id(0)
    n_assign = dest_ref.shape[0]
    tt = n_assign // TOP_K
    blk = zero_ref.shape[0]
    slot = i % 2

    def row_copy(a):
        t = i * tt + a // TOP_K
        return pltpu.make_async_copy(tok_ref.at[pl.ds(t * p, p), :],
                                     xs_ref.at[pl.ds(dest_ref[a] * p, p), :], sem.at[slot])

    def step_wait(s):
        pltpu.make_async_copy(tok_ref.at[pl.ds(0, n_assign * p), :],
                              xs_ref.at[pl.ds(0, n_assign * p), :], sem.at[s]).wait()

    def zero_row(row):
        return pltpu.make_async_copy(zero_ref.at[pl.ds(0, p), :], xs_ref.at[pl.ds(row * p, p), :], zsem)

    def zero_block(b):
        return pltpu.make_async_copy(zero_ref, xs_ref.at[pl.ds(b * blk, blk), :], zsem)

    lax.fori_loop(0, n_assign, lambda a, c: (row_copy(a).start(), c)[1], 0)

    @pl.when(i == 0)
    def _():
        zero_ref[...] = jnp.zeros_like(zero_ref)
        for e in range(n_exp):
            lo = pad_lo_ref[e]
            lax.fori_loop(0, pad_n_ref[e], lambda j, c: (zero_row(lo + j).start(), c)[1], 0)
        first = tail_ref[0]
        lax.fori_loop(0, tail_ref[1], lambda j, c: (zero_block(first + j).start(), c)[1], 0)
        for e in range(n_exp):
            lo = pad_lo_ref[e]
            lax.fori_loop(0, pad_n_ref[e], lambda j, c: (zero_row(lo + j).wait(), c)[1], 0)
        lax.fori_loop(0, tail_ref[1], lambda j, c: (zero_block(first + j).wait(), c)[1], 0)

    @pl.when(i > 0)
    def _():
        step_wait(1 - slot)

    @pl.when(i == pl.num_programs(0) - 1)
    def _():
        step_wait(slot)


def _dispatch(pad_lo, pad_n, tail, dest_flat, tok_slab, n_rows, d, tt, tme):
    p = d // LANES
    n = tok_slab.shape[0] // p
    n_exp = pad_lo.shape[0]
    assert tt * TOP_K <= n
    return pl.pallas_call(
        functools.partial(_dispatch_kernel, p=p, n_exp=n_exp),
        grid_spec=pltpu.PrefetchScalarGridSpec(
            num_scalar_prefetch=3,
            grid=(n // tt,),
            in_specs=[
                pl.BlockSpec((tt * TOP_K,), lambda i, lo, pn, tl: (i,), memory_space=pltpu.SMEM),
                pl.BlockSpec(memory_space=pl.ANY),
            ],
            out_specs=pl.BlockSpec(memory_space=pl.ANY),
            scratch_shapes=[pltpu.VMEM((tme * p, LANES), F32), pltpu.SemaphoreType.DMA((2,)),
                            pltpu.SemaphoreType.DMA],
        ),
        out_shape=jax.ShapeDtypeStruct((n_rows * p, LANES), F32),
        compiler_params=_params(("arbitrary",), 32),
        name="moe_dispatch",
    )(pad_lo, pad_n, tail, dest_flat, tok_slab)


def _expert_kernel(be_ref, nu_ref, xs_ref, w1g_ref, w1l_ref, b1g_ref, b1l_ref, w2_ref, b2_ref, y_ref, w2b_ref, *, p):
    i = pl.program_id(0)

    @pl.when((i == 0) | (be_ref[i] != be_ref[jnp.maximum(i - 1, 0)]))
    def _():
        w2b_ref[...] = w2_ref[...].astype(BF16)

    @pl.when(i < nu_ref[0])
    def _():
        tme = xs_ref.shape[0] // p
        x = _from_slab(xs_ref, tme, p).astype(BF16)
        hg = jnp.dot(x, w1g_ref[...], preferred_element_type=F32) + b1g_ref[...]
        hl = jnp.dot(x, w1l_ref[...], preferred_element_type=F32) + b1l_ref[...]
        g = jnp.minimum(hg, SWIGLU_LIMIT)
        lin = jnp.clip(hl, -SWIGLU_LIMIT, SWIGLU_LIMIT)
        act = (g * jax.nn.sigmoid(SWIGLU_ALPHA * g) * (lin + 1.0)).astype(BF16)
        y = jnp.dot(act, w2b_ref[...], preferred_element_type=F32) + b2_ref[...]
        _to_slab(y_ref, y, tme, p)

    @pl.when(i >= nu_ref[0])
    def _():
        y_ref[...] = jnp.zeros_like(y_ref)


def _experts(block_e, n_used, xs, w1g, w1l, b1g, b1l, w2, b2, tme):
    n_exp, d, ff = w1g.shape
    p = d // LANES
    n_rows = xs.shape[0] // p
    rows_in = lambda i, be, nu: (jnp.minimum(i, nu[0] - 1), 0)
    wsel = lambda i, be, nu: (be[i], 0, 0)
    return pl.pallas_call(
        functools.partial(_expert_kernel, p=p),
        grid_spec=pltpu.PrefetchScalarGridSpec(
            num_scalar_prefetch=2,
            grid=(n_rows // tme,),
            in_specs=[
                pl.BlockSpec((tme * p, LANES), rows_in),
                pl.BlockSpec((None, d, ff), wsel),
                pl.BlockSpec((None, d, ff), wsel),
                pl.BlockSpec((None, 1, ff), wsel),
                pl.BlockSpec((None, 1, ff), wsel),
                pl.BlockSpec((None, ff, d), wsel),
                pl.BlockSpec((None, 1, d), wsel),
            ],
            out_specs=pl.BlockSpec((tme * p, LANES), lambda i, be, nu: (i, 0)),
            scratch_shapes=[pltpu.VMEM((ff, d), BF16)],
        ),
        out_shape=jax.ShapeDtypeStruct((n_rows * p, LANES), F32),
        compiler_params=_params(("arbitrary",), 58),
        name="moe_experts",
    )(block_e, n_used, xs, w1g, w1l, b1g, b1l, w2, b2)


def _combine_kernel(dest_ref, dnext_ref, y_ref, gates_ref, x_ref, gate2_ref, gfin_ref, o_ref, buf_ref, sem,
                    *, p, final):
    i = pl.program_id(0)
    tt = x_ref.shape[0]
    slot = i % 2

    def fetch(d_ref, s):
        def body(t, c):
            for k in range(TOP_K):
                pltpu.make_async_copy(y_ref.at[pl.ds(d_ref[t * TOP_K + k] * p, p), :],
                                      buf_ref.at[s, k, pl.ds(t * p, p), :], sem.at[s]).start()
            return c
        lax.fori_loop(0, tt, body, 0)

    @pl.when(i == 0)
    def _():
        fetch(dest_ref, 0)

    @pl.when(i + 1 < pl.num_programs(0))
    def _():
        fetch(dnext_ref, 1 - slot)

    for k in range(TOP_K):
        pltpu.make_async_copy(y_ref.at[pl.ds(0, tt * p), :], buf_ref.at[slot, k], sem.at[slot]).wait()
    gates = gates_ref[...]
    acc = gates[:, 0:1] * _from_slab(buf_ref.at[slot, 0], tt, p)
    for k in range(1, TOP_K):
        acc = acc + gates[:, k:k + 1] * _from_slab(buf_ref.at[slot, k], tt, p)
    xn = x_ref[...] + gate2_ref[...] * acc
    if final:
        ms = jnp.mean(xn * xn, axis=-1, keepdims=True)
        xn = xn * lax.rsqrt(ms + EPS) * gfin_ref[...]
    o_ref[...] = xn


def _combine(dest_flat, y_slab, gates, x, gate2, g_final, seq, tt, final):
    n, d = x.shape
    p = d // LANES
    tiles = seq // tt
    n_tiles = n // tt
    return pl.pallas_call(
        functools.partial(_combine_kernel, p=p, final=final),
        grid=(n_tiles,),
        in_specs=[
            pl.BlockSpec((tt * TOP_K,), lambda i: (i,), memory_space=pltpu.SMEM),
            pl.BlockSpec((tt * TOP_K,), lambda i: (jnp.minimum(i + 1, n_tiles - 1),), memory_space=pltpu.SMEM),
            pl.BlockSpec(memory_space=pl.ANY),
            pl.BlockSpec((tt, TOP_K), lambda i: (i, 0)),
            pl.BlockSpec((tt, d), lambda i: (i, 0)),
            pl.BlockSpec((None, 1, d), lambda i: (i // tiles, 0, 0)),
            pl.BlockSpec((1, d), lambda i: (0, 0)),
        ],
        out_specs=pl.BlockSpec((tt, d), lambda i: (i, 0)),
        out_shape=jax.ShapeDtypeStruct((n, d), F32),
        scratch_shapes=[pltpu.VMEM((2, TOP_K, tt * p, LANES), F32), pltpu.SemaphoreType.DMA((2,))],
        compiler_params=_params(("arbitrary",), 40),
        name="moe_combine",
    )(dest_flat, dest_flat, y_slab, gates, x, gate2, g_final)


def _moe(x, tok_slab, logits_t, gate2, w1, b1, w2, b2, g_final, seq, final):
    n, d = x.shape
    n_exp = w1.shape[0]
    ff = w2.shape[1]
    tme = 256
    top_idx, gates, rank, counts = _route(logits_t, _row_tile(n, 256))
    counts = counts[:, 0]
    padded = (counts + tme - 1) // tme * tme
    pad_end = jnp.cumsum(padded)
    pad_start = pad_end - padded
    dest = (pad_start[top_idx] + rank).T.reshape(-1).astype(jnp.int32)
    n_rows = (n * TOP_K + tme - 1) // tme * tme + n_exp * tme
    n_blocks = n_rows // tme
    blk_start = jnp.arange(n_blocks, dtype=jnp.int32) * tme
    block_e = jnp.minimum(jnp.sum(blk_start[:, None] >= pad_end[None, :], axis=1), n_exp - 1).astype(jnp.int32)
    n_used = (pad_end[-1] // tme).astype(jnp.int32).reshape(1)
    pad_lo = (pad_start + counts).astype(jnp.int32)
    pad_n = (padded - counts).astype(jnp.int32)
    tail = jnp.concatenate([n_used, n_blocks - n_used]).astype(jnp.int32)

    xs = _dispatch(pad_lo, pad_n, tail, dest, tok_slab, n_rows, d, _row_tile(n, 256), tme)
    w1g, w1l = _w1_split(w1)
    b1g = b1[:, 0::2].reshape(n_exp, 1, ff)
    b1l = b1[:, 1::2].reshape(n_exp, 1, ff)
    y = _experts(block_e, n_used, xs, w1g, w1l, b1g, b1l, w2, b2.reshape(n_exp, 1, d), tme)
    return _combine(dest, y, gates.T, x, gate2, g_final.reshape(1, d), seq, _row_tile(seq, 128), final)


def kernel(x, c, ctx, c_ctx, w_mod, b_mod, g_mix, g_ffn, ab_w_in, ab_q_gain, ab_k_gain, ab_w_out,
           cv_w_in, cv_b_in, cv_w_dw, cv_b_dw, cv_ln_g, cv_ln_b, cv_w_out, cv_b_out,
           moe_w_router, moe_b_router, moe_w1, moe_b1, moe_w2, moe_b2, g_final):
    bsz, seq, d = x.shape
    ctx_len = ctx.shape[1]
    depth = w_mod.shape[0]
    n = bsz * seq
    assert bsz + 1 <= MOD_ROWS and d % LANES == 0
    tm = _row_tile(seq, 256)
    assert ctx_len % tm == 0

    cond = jnp.zeros((MOD_ROWS, d), F32).at[:bsz].set(c).at[bsz].set(c_ctx)
    mod = _modulation(cond, w_mod, b_mod).reshape(depth, MOD_ROWS, N_MOD, d)

    xf = x.reshape(n, d)
    for i in range(depth):
        j = i // 2
        sh1, sc1, gt1, sh2, sc2, gt2 = (mod[i, :, m].reshape(MOD_ROWS, 1, d) for m in range(N_MOD))
        a1 = g_mix[i] * (1.0 + sc1)
        a2 = g_ffn[i] * (1.0 + sc2)
        router = _router_operands(moe_w_router[i], moe_b_router[i])
        if i % 2 == 0:
            assert not any(k % 2 == 0 for k in range(i + 1, depth)), "context update path not needed at this depth"
            xin = jnp.concatenate([xf.reshape(bsz, seq, d), ctx], axis=1)
            cos_t, sin_t = _rope_tables(seq, ctx_len)
            q, k, v, f = _ab_in(xin, a1, sh1, ab_w_in[j].astype(BF16), ab_q_gain[j].reshape(1, -1),
                                ab_k_gain[j].reshape(1, -1), cos_t, sin_t, seq, tm)
            attn = _attention(q, k, v, seq, tm)
            fmix = _fourier_mix(f, seq)
            w_out = ab_w_out[j].astype(BF16)
            xf, tok, logits = _ab_out(attn.reshape(n, ATTN_WIDTH), fmix.reshape(n, FOURIER_WIDTH),
                                      w_out[:ATTN_WIDTH], w_out[ATTN_WIDTH:], xf, gt1, a2, sh2,
                                      router, seq, tm)
        else:
            u = _cv_in(xf, a1, sh1, cv_w_in[j].astype(BF16), cv_b_in[j].reshape(1, -1), seq, tm)
            v = _dwconv(u.reshape(bsz, seq, d), cv_w_dw[j], cv_b_dw[j].reshape(1, -1),
                        _row_tile(seq, 128), _row_tile(d, 512))
            xf, tok, logits = _cv_out(v.reshape(n, d), cv_ln_g[j].reshape(1, -1), cv_ln_b[j].reshape(1, -1),
                                      cv_w_out[j].astype(BF16), cv_b_out[j].reshape(1, -1), xf, gt1, a2, sh2,
                                      router, seq, tm)
        xf = _moe(xf, tok, logits, gt2, moe_w1[i], moe_b1[i], moe_w2[i], moe_b2[i], g_final, seq,
                  final=(i == depth - 1))
    return xf.reshape(bsz, seq, d)
```

```python
import functools

import numpy as np
import jax
import jax.numpy as jnp
from jax import lax
from jax.experimental import pallas as pl
from jax.experimental.pallas import tpu as pltpu

F32 = jnp.float32
BF16 = jnp.bfloat16
HIGHEST = lax.Precision.HIGHEST

LANES = 128
SUBLANES = 8
HEAD_DIM = 128
N_Q_HEADS = 8
N_KV_HEADS = 2
Q_PER_KV = N_Q_HEADS // N_KV_HEADS
ATTN_WIDTH = N_Q_HEADS * HEAD_DIM
KV_WIDTH = N_KV_HEADS * HEAD_DIM
GRID_W = 64
ROPE_AXIS_DIM = HEAD_DIM // 2
ROPE_THETA = 10000.0
FOURIER_GROUPS = 8
FOURIER_GROUP_DIM = 128
FOURIER_WIDTH = FOURIER_GROUPS * FOURIER_GROUP_DIM
TOP_K = 4
SWIGLU_LIMIT = 7.0
SWIGLU_ALPHA = 1.702
N_MOD = 6
EPS = 1e-6
MOD_ROWS = 8
HALO = 16
MIB = 1024 * 1024


def _params(sem, vmem_mib):
    return pltpu.CompilerParams(dimension_semantics=sem, vmem_limit_bytes=vmem_mib * MIB)


def _row_tile(n, want):
    t = min(n, want)
    assert n % t == 0, (n, t)
    return t


def _resident(shape, index_map):
    return pl.BlockSpec(shape, index_map, pipeline_mode=pl.Buffered(1))


def _mod_kernel(c_ref, w_ref, b_ref, o_ref):
    c = c_ref[...]
    s = c * jax.nn.sigmoid(c)
    o_ref[...] = jnp.dot(s, w_ref[...], preferred_element_type=F32, precision=HIGHEST) + b_ref[...]


def _modulation(cond, w_mod, b_mod):
    depth, d, n = w_mod.shape
    tn = _row_tile(n, min(1024, d))
    return pl.pallas_call(
        _mod_kernel,
        grid=(depth, n // tn),
        in_specs=[
            pl.BlockSpec((MOD_ROWS, d), lambda i, j: (0, 0)),
            pl.BlockSpec((None, d, tn), lambda i, j: (i, 0, j)),
            pl.BlockSpec((None, 1, tn), lambda i, j: (i, 0, j)),
        ],
        out_specs=pl.BlockSpec((None, MOD_ROWS, tn), lambda i, j: (i, 0, j)),
        out_shape=jax.ShapeDtypeStruct((depth, MOD_ROWS, n), F32),
        compiler_params=_params(("arbitrary", "arbitrary"), 40),
        name="adaln_modulation",
    )(cond, w_mod, b_mod.reshape(depth, 1, n))


def _norm_mod(x, a, b):
    ms = jnp.mean(x * x, axis=-1, keepdims=True)
    return x * lax.rsqrt(ms + EPS) * a + b


def _to_slab(ref, val, rows, p):
    for j in range(p):
        ref[pl.ds(j, rows, stride=p), :] = val[:, j * LANES:(j + 1) * LANES]


def _from_slab(ref, rows, p):
    return jnp.concatenate([ref[pl.ds(j, rows, stride=p), :] for j in range(p)], axis=1)


def _split_bf16(v):
    hi = v.astype(BF16)
    return hi, (v - hi.astype(F32)).astype(BF16)


def _residual_epilogue(mix, x_ref, gate_ref, a2_ref, b2_ref, wrh_ref, wrl_ref, br_ref, xo_ref, tok_ref, lg_ref):
    tm, d = x_ref.shape
    xn = x_ref[...] + gate_ref[...] * mix
    xo_ref[...] = xn
    tok = _norm_mod(xn, a2_ref[...], b2_ref[...])
    tok_hi, tok_lo = _split_bf16(tok)
    nt = (((1,), (1,)), ((), ()))
    wrh = wrh_ref[...]
    lg = (lax.dot_general(wrh, tok_hi, nt, preferred_element_type=F32)
          + lax.dot_general(wrh, tok_lo, nt, preferred_element_type=F32)
          + lax.dot_general(wrl_ref[...], tok_hi, nt, preferred_element_type=F32))
    lg_ref[...] = lg + br_ref[...]
    _to_slab(tok_ref, tok, tm, d // LANES)


def _epilogue_specs(tm, d, n_exp, tiles_per_batch):
    bmap = lambda i: (i // tiles_per_batch, 0, 0)
    in_specs = [
        pl.BlockSpec((tm, d), lambda i: (i, 0)),
        pl.BlockSpec((None, 1, d), bmap),
        pl.BlockSpec((None, 1, d), bmap),
        pl.BlockSpec((None, 1, d), bmap),
        _resident((n_exp, d), lambda i: (0, 0)),
        _resident((n_exp, d), lambda i: (0, 0)),
        _resident((n_exp, 1), lambda i: (0, 0)),
    ]
    p = d // LANES
    out_specs = [
        pl.BlockSpec((tm, d), lambda i: (i, 0)),
        pl.BlockSpec((tm * p, LANES), lambda i: (i, 0)),
        pl.BlockSpec((n_exp, tm), lambda i: (0, i)),
    ]
    return in_specs, out_specs


def _epilogue_out_shape(n, d, n_exp):
    return [
        jax.ShapeDtypeStruct((n, d), F32),
        jax.ShapeDtypeStruct((n * (d // LANES), LANES), F32),
        jax.ShapeDtypeStruct((n_exp, n), F32),
    ]


def _router_operands(w_router, b_router):
    wt = w_router.T
    hi = wt.astype(BF16)
    lo = (wt - hi.astype(F32)).astype(BF16)
    return hi, lo, b_router.reshape(-1, 1)


def _ab_in_kernel(x_ref, a_ref, b_ref, w_ref, qg_ref, kg_ref, cos_ref, sin_ref,
                  q_ref, k_ref, v_ref, f_ref):
    h = _norm_mod(x_ref[...], a_ref[...], b_ref[...]).astype(BF16)
    y = jnp.dot(h, w_ref[...], preferred_element_type=F32)
    cos = cos_ref[...]
    sin = sin_ref[...]
    lane = lax.broadcasted_iota(jnp.int32, cos.shape, 1)
    low = (lane % (2 * (ROPE_AXIS_DIM // 2))) < (ROPE_AXIS_DIM // 2)

    def head(col, gain, scale):
        blk = y[:, col:col + HEAD_DIM]
        ms = jnp.mean(blk * blk, axis=-1, keepdims=True)
        r = blk * lax.rsqrt(ms + EPS) * gain
        up = pltpu.roll(r, HEAD_DIM - ROPE_AXIS_DIM // 2, axis=1)
        dn = pltpu.roll(r, ROPE_AXIS_DIM // 2, axis=1)
        return (r * cos + jnp.where(low, up, dn) * sin) * scale

    for hd in range(N_Q_HEADS):
        q_ref[:, hd * HEAD_DIM:(hd + 1) * HEAD_DIM] = head(
            hd * HEAD_DIM, qg_ref[...], HEAD_DIM ** -0.5).astype(BF16)
    for hd in range(N_KV_HEADS):
        k_ref[:, hd * HEAD_DIM:(hd + 1) * HEAD_DIM] = head(
            ATTN_WIDTH + hd * HEAD_DIM, kg_ref[...], 1.0).astype(BF16)
    v_ref[...] = y[:, ATTN_WIDTH + KV_WIDTH:ATTN_WIDTH + 2 * KV_WIDTH].astype(BF16)
    f_ref[...] = y[:, ATTN_WIDTH + 2 * KV_WIDTH:].astype(BF16)


def _ab_in(xin, a_mod, b_mod, w_in, q_gain, k_gain, cos_t, sin_t, seq, tm):
    bsz, lt, d = xin.shape
    n_out = w_in.shape[1]
    lat_tiles = seq // tm
    mmap = lambda b, t: (jnp.where(t < lat_tiles, b, bsz), 0, 0)
    row = lambda b, t: (b, t, 0)
    return pl.pallas_call(
        _ab_in_kernel,
        grid=(bsz, lt // tm),
        in_specs=[
            pl.BlockSpec((None, tm, d), row),
            pl.BlockSpec((None, 1, d), mmap),
            pl.BlockSpec((None, 1, d), mmap),
            _resident((d, n_out), lambda b, t: (0, 0)),
            _resident((1, HEAD_DIM), lambda b, t: (0, 0)),
            _resident((1, HEAD_DIM), lambda b, t: (0, 0)),
            pl.BlockSpec((tm, HEAD_DIM), lambda b, t: (t, 0)),
            pl.BlockSpec((tm, HEAD_DIM), lambda b, t: (t, 0)),
        ],
        out_specs=[
            pl.BlockSpec((None, tm, ATTN_WIDTH), row),
            pl.BlockSpec((None, tm, KV_WIDTH), row),
            pl.BlockSpec((None, tm, KV_WIDTH), row),
            pl.BlockSpec((None, tm, FOURIER_WIDTH), row),
        ],
        out_shape=[
            jax.ShapeDtypeStruct((bsz, lt, ATTN_WIDTH), BF16),
            jax.ShapeDtypeStruct((bsz, lt, KV_WIDTH), BF16),
            jax.ShapeDtypeStruct((bsz, lt, KV_WIDTH), BF16),
            jax.ShapeDtypeStruct((bsz, lt, FOURIER_WIDTH), BF16),
        ],
        compiler_params=_params(("arbitrary", "arbitrary"), 48),
        name="ab_in_proj",
    )(xin, a_mod, b_mod, w_in, q_gain, k_gain, cos_t, sin_t)


def _rope_tables(seq, ctx_len):
    pos = np.arange(seq)
    inv_freq = ROPE_THETA ** (-np.arange(0, ROPE_AXIS_DIM, 2, dtype=np.float32) / ROPE_AXIS_DIM)
    ang_r = (pos // GRID_W).astype(np.float32)[:, None] * inv_freq.astype(np.float32)
    ang_c = (pos % GRID_W).astype(np.float32)[:, None] * inv_freq.astype(np.float32)
    ang_r = jnp.asarray(ang_r, F32)
    ang_c = jnp.asarray(ang_c, F32)
    cos = jnp.concatenate([jnp.cos(ang_r), jnp.cos(ang_r), jnp.cos(ang_c), jnp.cos(ang_c)], axis=1)
    sin = jnp.concatenate([-jnp.sin(ang_r), jnp.sin(ang_r), -jnp.sin(ang_c), jnp.sin(ang_c)], axis=1)
    cos = jnp.concatenate([cos, jnp.ones((ctx_len, HEAD_DIM), F32)], axis=0)
    sin = jnp.concatenate([sin, jnp.zeros((ctx_len, HEAD_DIM), F32)], axis=0)
    return cos, sin


def _attn_kernel(q_ref, k_ref, v_ref, o_ref):
    k = k_ref[...]
    v = v_ref[...]
    for g in range(Q_PER_KV):
        q = q_ref[:, g * HEAD_DIM:(g + 1) * HEAD_DIM]
        s = lax.dot_general(q, k, (((1,), (1,)), ((), ())), preferred_element_type=F32)
        m = jnp.max(s, axis=-1, keepdims=True)
        p = jnp.exp(s - m)
        l = jnp.sum(p, axis=-1, keepdims=True)
        o = jnp.dot(p.astype(BF16), v, preferred_element_type=F32)
        o_ref[:, g * HEAD_DIM:(g + 1) * HEAD_DIM] = (o / l).astype(BF16)


def _attention(q, k, v, seq, tq):
    bsz, lt, _ = q.shape
    gw = Q_PER_KV * HEAD_DIM
    return pl.pallas_call(
        _attn_kernel,
        grid=(bsz, N_KV_HEADS, seq // tq),
        in_specs=[
            pl.BlockSpec((None, tq, gw), lambda b, h, i: (b, i, h)),
            pl.BlockSpec((None, lt, HEAD_DIM), lambda b, h, i: (b, 0, h)),
            pl.BlockSpec((None, lt, HEAD_DIM), lambda b, h, i: (b, 0, h)),
        ],
        out_specs=pl.BlockSpec((None, tq, gw), lambda b, h, i: (b, i, h)),
        out_shape=jax.ShapeDtypeStruct((bsz, seq, ATTN_WIDTH), BF16),
        compiler_params=_params(("arbitrary", "arbitrary", "arbitrary"), 48),
        name="attention",
    )(q, k, v)


def _fourier_chan_kernel(f_ref, cr_ref, ci_ref, zr_ref, zi_ref):
    for g in range(FOURIER_GROUPS):
        cols = slice(g * FOURIER_GROUP_DIM, (g + 1) * FOURIER_GROUP_DIM)
        u = f_ref[:, cols]
        zr_ref[:, cols] = jnp.dot(u, cr_ref[...], preferred_element_type=F32)
        zi_ref[:, cols] = jnp.dot(u, ci_ref[...], preferred_element_type=F32)


def _fourier_outer_kernel(zr_ref, zi_ref, fr_ref, fi_ref, yr_ref, yi_ref):
    fr = fr_ref[...]
    fi = fi_ref[...]
    for jj in range(zr_ref.shape[1]):
        zr = zr_ref[:, jj, :].astype(BF16)
        zi = zi_ref[:, jj, :].astype(BF16)
        yr_ref[:, jj, :] = (jnp.dot(fr, zr, preferred_element_type=F32)
                            - jnp.dot(fi, zi, preferred_element_type=F32))
        yi_ref[:, jj, :] = (jnp.dot(fr, zi, preferred_element_type=F32)
                            + jnp.dot(fi, zr, preferred_element_type=F32))


def _fourier_inner_kernel(yr_ref, yi_ref, twr_ref, twi_ref, gr_ref, gi_ref, o_ref, *, scale):
    gr = gr_ref[...]
    gi = gi_ref[...]
    for kk in range(yr_ref.shape[0]):
        yr = yr_ref[kk]
        yi = yi_ref[kk]
        twr = twr_ref[kk]
        twi = twi_ref[kk]
        tr = (twr * yr - twi * yi).astype(BF16)
        ti = (twr * yi + twi * yr).astype(BF16)
        x = jnp.dot(gr, tr, preferred_element_type=F32) - jnp.dot(gi, ti, preferred_element_type=F32)
        o_ref[:, kk, :] = x * scale


def _dft_parts(n):
    jk = np.outer(np.arange(n), np.arange(n)) % n
    ang = 2 * np.pi * jk / n
    return np.cos(ang), -np.sin(ang)


def _fourier_mix(f, seq):
    bsz = f.shape[0]
    gd = FOURIER_GROUP_DIM
    width = FOURIER_WIDTH
    cr, ci = _dft_parts(gd)
    tm = _row_tile(seq, 512)
    zr, zi = pl.pallas_call(
        _fourier_chan_kernel,
        grid=(bsz, seq // tm),
        in_specs=[
            pl.BlockSpec((None, tm, width), lambda b, i: (b, i, 0)),
            _resident((gd, gd), lambda b, i: (0, 0)),
            _resident((gd, gd), lambda b, i: (0, 0)),
        ],
        out_specs=[pl.BlockSpec((None, tm, width), lambda b, i: (b, i, 0))] * 2,
        out_shape=[jax.ShapeDtypeStruct((bsz, seq, width), F32)] * 2,
        compiler_params=_params(("arbitrary", "arbitrary"), 32),
        name="fourier_channels",
    )(f, jnp.asarray(cr, BF16), jnp.asarray(ci, BF16))

    n_b = min(seq, 64)
    n_a = seq // n_b
    g2 = min(n_b, SUBLANES)
    fr, fi = _dft_parts(n_a)
    split = lambda b, j: (b, 0, j, 0)
    yr, yi = pl.pallas_call(
        _fourier_outer_kernel,
        grid=(bsz, n_b // g2),
        in_specs=[
            pl.BlockSpec((None, n_a, g2, width), split),
            pl.BlockSpec((None, n_a, g2, width), split),
            _resident((n_a, n_a), lambda b, j: (0, 0)),
            _resident((n_a, n_a), lambda b, j: (0, 0)),
        ],
        out_specs=[pl.BlockSpec((None, n_a, g2, width), split)] * 2,
        out_shape=[jax.ShapeDtypeStruct((bsz, n_a, n_b, width), F32)] * 2,
        compiler_params=_params(("arbitrary", "arbitrary"), 40),
        name="fourier_outer_dft",
    )(zr.reshape(bsz, n_a, n_b, width), zi.reshape(bsz, n_a, n_b, width),
      jnp.asarray(fr, BF16), jnp.asarray(fi, BF16))

    ang = 2 * np.pi * np.outer(np.arange(n_a), np.arange(n_b)) / seq
    twr = jnp.asarray(np.cos(ang)[:, :, None], F32)
    twi = jnp.asarray(-np.sin(ang)[:, :, None], F32)
    gr, gi = _dft_parts(n_b)
    g1 = min(n_a, 16)
    out = pl.pallas_call(
        functools.partial(_fourier_inner_kernel, scale=float((seq * gd) ** -0.5)),
        grid=(bsz, n_a // g1),
        in_specs=[
            pl.BlockSpec((None, g1, n_b, width), lambda b, j: (b, j, 0, 0)),
            pl.BlockSpec((None, g1, n_b, width), lambda b, j: (b, j, 0, 0)),
            pl.BlockSpec((g1, n_b, 1), lambda b, j: (j, 0, 0)),
            pl.BlockSpec((g1, n_b, 1), lambda b, j: (j, 0, 0)),
            _resident((n_b, n_b), lambda b, j: (0, 0)),
            _resident((n_b, n_b), lambda b, j: (0, 0)),
        ],
        out_specs=pl.BlockSpec((None, n_b, g1, width), lambda b, j: (b, 0, j, 0)),
        out_shape=jax.ShapeDtypeStruct((bsz, n_b, n_a, width), F32),
        compiler_params=_params(("arbitrary", "arbitrary"), 48),
        name="fourier_inner_dft",
    )(yr, yi, twr, twi, jnp.asarray(gr, BF16), jnp.asarray(gi, BF16))
    return out.reshape(bsz, seq, width)


def _ab_out_kernel(at_ref, fm_ref, wa_ref, wf_ref, x_ref, gate_ref, a2_ref, b2_ref, wrh_ref, wrl_ref, br_ref,
                   xo_ref, tok_ref, lg_ref):
    mix = (jnp.dot(at_ref[...], wa_ref[...], preferred_element_type=F32)
           + jnp.dot(fm_ref[...].astype(BF16), wf_ref[...], preferred_element_type=F32))
    _residual_epilogue(mix, x_ref, gate_ref, a2_ref, b2_ref, wrh_ref, wrl_ref, br_ref, xo_ref, tok_ref, lg_ref)


def _ab_out(attn, fmix, w_attn, w_four, x, gate, a2, b2, router, seq, tm):
    n, d = x.shape
    n_exp = router[0].shape[0]
    ep_in, ep_out = _epilogue_specs(tm, d, n_exp, seq // tm)
    return pl.pallas_call(
        _ab_out_kernel,
        grid=(n // tm,),
        in_specs=[
            pl.BlockSpec((tm, ATTN_WIDTH), lambda i: (i, 0)),
            pl.BlockSpec((tm, FOURIER_WIDTH), lambda i: (i, 0)),
            _resident((ATTN_WIDTH, d), lambda i: (0, 0)),
            _resident((FOURIER_WIDTH, d), lambda i: (0, 0)),
        ] + ep_in,
        out_specs=ep_out,
        out_shape=_epilogue_out_shape(n, d, n_exp),
        compiler_params=_params(("arbitrary",), 48),
        name="ab_out_proj",
    )(attn, fmix, w_attn, w_four, x, gate, a2, b2, *router)


def _cv_in_kernel(x_ref, a_ref, b_ref, w_ref, bias_ref, u_ref):
    d = x_ref.shape[1]
    h = _norm_mod(x_ref[...], a_ref[...], b_ref[...]).astype(BF16)
    y = jnp.dot(h, w_ref[...], preferred_element_type=F32) + bias_ref[...]
    u_ref[...] = y[:, :d] * jax.nn.sigmoid(y[:, d:])


def _cv_in(x, a_mod, b_mod, w_in, b_in, seq, tm):
    n, d = x.shape
    tiles = seq // tm
    bmap = lambda i: (i // tiles, 0, 0)
    return pl.pallas_call(
        _cv_in_kernel,
        grid=(n // tm,),
        in_specs=[
            pl.BlockSpec((tm, d), lambda i: (i, 0)),
            pl.BlockSpec((None, 1, d), bmap),
            pl.BlockSpec((None, 1, d), bmap),
            _resident((d, 2 * d), lambda i: (0, 0)),
            _resident((1, 2 * d), lambda i: (0, 0)),
        ],
        out_specs=pl.BlockSpec((tm, d), lambda i: (i, 0)),
        out_shape=jax.ShapeDtypeStruct((n, d), F32),
        compiler_params=_params(("arbitrary",), 48),
        name="conv_in_proj_glu",
    )(x, a_mod, b_mod, w_in, b_in)


def _dwconv_kernel(prev_ref, cur_ref, next_ref, w_ref, b_ref, o_ref, win_ref, *, window, row_chunk):
    i = pl.program_id(1)
    tm, tc = cur_ref.shape
    first = i == 0
    last = i == pl.num_programs(1) - 1
    win_ref[0:HALO, :] = jnp.where(first, 0.0, prev_ref[...])
    win_ref[HALO:HALO + tm, :] = cur_ref[...]
    win_ref[HALO + tm:, :] = jnp.where(last, 0.0, next_ref[...])
    lead = HALO - window // 2
    ext = row_chunk + SUBLANES
    assert tm % row_chunk == 0 and lead + window - 1 < 4 * SUBLANES and ext + 3 * SUBLANES <= row_chunk + 2 * HALO
    for c in range(tc // LANES):
        cols = slice(c * LANES, (c + 1) * LANES)
        for r in range(tm // row_chunk):
            r0 = r * row_chunk
            acc = jnp.zeros((row_chunk, LANES), F32) + b_ref[:, cols]
            for b in range(SUBLANES):
                part = None
                for a in range(4):
                    j = SUBLANES * a + b - lead
                    if 0 <= j < window:
                        lo = r0 + SUBLANES * a
                        term = w_ref[j:j + 1, cols] * win_ref[lo:lo + ext, cols]
                        part = term if part is None else part + term
                acc = acc + part[b:b + row_chunk]
            o_ref[r0:r0 + row_chunk, cols] = acc


def _dwconv(u, w_dw, b_dw, tm, tc):
    bsz, seq, d = u.shape
    window = w_dw.shape[0]
    assert window // 2 <= HALO and tm % HALO == 0
    hb = tm // HALO
    n_halo = seq // HALO
    return pl.pallas_call(
        functools.partial(_dwconv_kernel, window=window, row_chunk=min(tm, 64)),
        grid=(bsz, seq // tm, d // tc),
        in_specs=[
            pl.BlockSpec((None, HALO, tc), lambda b, i, c: (b, jnp.maximum(i * hb - 1, 0), c)),
            pl.BlockSpec((None, tm, tc), lambda b, i, c: (b, i, c)),
            pl.BlockSpec((None, HALO, tc), lambda b, i, c: (b, jnp.minimum((i + 1) * hb, n_halo - 1), c)),
            pl.BlockSpec((window, tc), lambda b, i, c: (0, c)),
            pl.BlockSpec((1, tc), lambda b, i, c: (0, c)),
        ],
        out_specs=pl.BlockSpec((None, tm, tc), lambda b, i, c: (b, i, c)),
        out_shape=jax.ShapeDtypeStruct((bsz, seq, d), F32),
        scratch_shapes=[pltpu.VMEM((tm + 2 * HALO, tc), F32)],
        compiler_params=_params(("arbitrary", "arbitrary", "arbitrary"), 32),
        name="depthwise_conv",
    )(u, u, u, w_dw, b_dw)


def _cv_out_kernel(v_ref, lng_ref, lnb_ref, w_ref, bo_ref, x_ref, gate_ref, a2_ref, b2_ref, wrh_ref, wrl_ref, br_ref,
                   xo_ref, tok_ref, lg_ref):
    v = v_ref[...]
    mu = jnp.mean(v, axis=-1, keepdims=True)
    vc = v - mu
    var = jnp.mean(vc * vc, axis=-1, keepdims=True)
    y = vc * lax.rsqrt(var + EPS) * lng_ref[...] + lnb_ref[...]
    y = (y * jax.nn.sigmoid(y)).astype(BF16)
    mix = jnp.dot(y, w_ref[...], preferred_element_type=F32) + bo_ref[...]
    _residual_epilogue(mix, x_ref, gate_ref, a2_ref, b2_ref, wrh_ref, wrl_ref, br_ref, xo_ref, tok_ref, lg_ref)


def _cv_out(v, ln_g, ln_b, w_out, b_out, x, gate, a2, b2, router, seq, tm):
    n, d = x.shape
    n_exp = router[0].shape[0]
    ep_in, ep_out = _epilogue_specs(tm, d, n_exp, seq // tm)
    return pl.pallas_call(
        _cv_out_kernel,
        grid=(n // tm,),
        in_specs=[
            pl.BlockSpec((tm, d), lambda i: (i, 0)),
            _resident((1, d), lambda i: (0, 0)),
            _resident((1, d), lambda i: (0, 0)),
            _resident((d, d), lambda i: (0, 0)),
            _resident((1, d), lambda i: (0, 0)),
        ] + ep_in,
        out_specs=ep_out,
        out_shape=_epilogue_out_shape(n, d, n_exp),
        compiler_params=_params(("arbitrary",), 48),
        name="conv_out_proj",
    )(v, ln_g, ln_b, w_out, b_out, x, gate, a2, b2, *router)


def _route_kernel(lg_ref, idx_ref, gate_ref, rank_ref, cnt_ref, run_ref):
    i = pl.program_id(0)

    @pl.when(i == 0)
    def _():
        run_ref[...] = jnp.zeros_like(run_ref)

    lg = lg_ref[...]
    n_exp, tr = lg.shape
    sub = lax.broadcasted_iota(jnp.int32, (n_exp, tr), 0).astype(F32)
    work = lg
    vals, idxs = [], []
    member = jnp.zeros((n_exp, tr), F32)
    for _ in range(TOP_K):
        m = jnp.max(work, axis=0, keepdims=True)
        sel = jnp.min(jnp.where(work == m, sub, float(n_exp)), axis=0, keepdims=True)
        hit = sub == sel
        vals.append(m)
        idxs.append(sel)
        member = jnp.where(hit, 1.0, member)
        work = jnp.where(hit, -jnp.inf, work)
    es = [jnp.exp(v - vals[0]) for v in vals]
    tot = es[0] + es[1] + es[2] + es[3]
    r_i = lax.broadcasted_iota(jnp.int32, (tr, tr), 0)
    c_i = lax.broadcasted_iota(jnp.int32, (tr, tr), 1)
    earlier = (r_i < c_i).astype(BF16)
    before = jnp.dot(member.astype(BF16), earlier, preferred_element_type=F32) + run_ref[...]
    for k in range(TOP_K):
        idx_ref[k:k + 1, :] = idxs[k].astype(jnp.int32)
        gate_ref[k:k + 1, :] = es[k] / tot
        rank_ref[k:k + 1, :] = jnp.sum(jnp.where(sub == idxs[k], before, 0.0),
                                       axis=0, keepdims=True).astype(jnp.int32)
    run_ref[...] += jnp.sum(member, axis=1, keepdims=True)
    cnt_ref[...] = run_ref[...].astype(jnp.int32)


def _route(logits_t, tr):
    n_exp, n = logits_t.shape
    tok = lambda i: (0, i)
    return pl.pallas_call(
        _route_kernel,
        grid=(n // tr,),
        in_specs=[pl.BlockSpec((n_exp, tr), tok)],
        out_specs=[
            pl.BlockSpec((TOP_K, tr), tok),
            pl.BlockSpec((TOP_K, tr), tok),
            pl.BlockSpec((TOP_K, tr), tok),
            pl.BlockSpec((n_exp, 1), lambda i: (0, 0)),
        ],
        out_shape=[
            jax.ShapeDtypeStruct((TOP_K, n), jnp.int32),
            jax.ShapeDtypeStruct((TOP_K, n), F32),
            jax.ShapeDtypeStruct((TOP_K, n), jnp.int32),
            jax.ShapeDtypeStruct((n_exp, 1), jnp.int32),
        ],
        scratch_shapes=[pltpu.VMEM((n_exp, 1), F32)],
        compiler_params=_params(("arbitrary",), 32),
        name="moe_route",
    )(logits_t)


def _w1_split_kernel(w_ref, perm_ref, g_ref, l_ref):
    wide = 2 * LANES
    for c in range(w_ref.shape[1] // wide):
        t = jnp.dot(w_ref[:, c * wide:(c + 1) * wide].astype(BF16), perm_ref[...], preferred_element_type=F32)
        g_ref[:, c * LANES:(c + 1) * LANES] = t[:, :LANES].astype(BF16)
        l_ref[:, c * LANES:(c + 1) * LANES] = t[:, LANES:].astype(BF16)


def _w1_split(w1_all, layer):
    _, n_exp, d, ff2 = w1_all.shape
    ff = ff2 // 2
    assert ff % LANES == 0
    perm = np.zeros((2 * LANES, 2 * LANES), np.float32)
    perm[2 * np.arange(LANES), np.arange(LANES)] = 1.0
    perm[2 * np.arange(LANES) + 1, LANES + np.arange(LANES)] = 1.0
    tk = _row_tile(d, 512)
    return pl.pallas_call(
        _w1_split_kernel,
        grid=(n_exp, d // tk),
        in_specs=[
            pl.BlockSpec((None, None, tk, ff2), lambda e, r: (layer, e, r, 0)),
            _resident((2 * LANES, 2 * LANES), lambda e, r: (0, 0)),
        ],
        out_specs=[pl.BlockSpec((None, tk, ff), lambda e, r: (e, r, 0))] * 2,
        out_shape=[jax.ShapeDtypeStruct((n_exp, d, ff), BF16)] * 2,
        compiler_params=_params(("arbitrary", "arbitrary"), 32),
        name="moe_w1_split",
    )(w1_all, jnp.asarray(perm, BF16))


def _dispatch_kernel(pad_lo_ref, pad_n_ref, tail_ref, dest_ref, tok_ref, xs_ref, zero_ref, sem, zsem, *, p, n_exp):
    i = pl.program_id(0)
    n_assign = dest_ref.shape[0]
    blk = zero_ref.shape[0]

    def row_copy(a):
        t = a // TOP_K
        return pltpu.make_async_copy(tok_ref.at[pl.ds(t * p, p), :],
                                     xs_ref.at[pl.ds(dest_ref[a] * p, p), :], sem)

    def zero_row(row):
        return pltpu.make_async_copy(zero_ref.at[pl.ds(0, p), :], xs_ref.at[pl.ds(row * p, p), :], zsem)

    def zero_block(b):
        return pltpu.make_async_copy(zero_ref, xs_ref.at[pl.ds(b * blk, blk), :], zsem)

    lax.fori_loop(0, n_assign, lambda a, c: (row_copy(a).start(), c)[1], 0)

    @pl.when(i == 0)
    def _():
        zero_ref[...] = jnp.zeros_like(zero_ref)
        for e in range(n_exp):
            lo = pad_lo_ref[e]
            lax.fori_loop(0, pad_n_ref[e], lambda j, c: (zero_row(lo + j).start(), c)[1], 0)
        first = tail_ref[0]
        lax.fori_loop(0, tail_ref[1], lambda j, c: (zero_block(first + j).start(), c)[1], 0)
        for e in range(n_exp):
            lo = pad_lo_ref[e]
            lax.fori_loop(0, pad_n_ref[e], lambda j, c: (zero_row(lo + j).wait(), c)[1], 0)
        lax.fori_loop(0, tail_ref[1], lambda j, c: (zero_block(first + j).wait(), c)[1], 0)

    for _ in range(TOP_K):
        pltpu.make_async_copy(tok_ref, xs_ref.at[pl.ds(0, tok_ref.shape[0]), :], sem).wait()


def _dispatch(pad_lo, pad_n, tail, dest_flat, tok_slab, n_rows, d, tt, tme):
    p = d // LANES
    n = tok_slab.shape[0] // p
    n_exp = pad_lo.shape[0]
    return pl.pallas_call(
        functools.partial(_dispatch_kernel, p=p, n_exp=n_exp),
        grid_spec=pltpu.PrefetchScalarGridSpec(
            num_scalar_prefetch=3,
            grid=(n // tt,),
            in_specs=[
                pl.BlockSpec((tt * TOP_K,), lambda i, lo, pn, tl: (i,), memory_space=pltpu.SMEM),
                pl.BlockSpec((tt * p, LANES), lambda i, lo, pn, tl: (i, 0)),
            ],
            out_specs=pl.BlockSpec(memory_space=pl.ANY),
            scratch_shapes=[pltpu.VMEM((tme * p, LANES), F32), pltpu.SemaphoreType.DMA,
                            pltpu.SemaphoreType.DMA],
        ),
        out_shape=jax.ShapeDtypeStruct((n_rows * p, LANES), F32),
        compiler_params=_params(("arbitrary",), 32),
        name="moe_dispatch",
    )(pad_lo, pad_n, tail, dest_flat, tok_slab)


def _expert_kernel(be_ref, nu_ref, xs_ref, w1g_ref, w1l_ref, b1g_ref, b1l_ref, w2_ref, b2_ref, y_ref, w2b_ref, *, p):
    i = pl.program_id(0)

    @pl.when((i == 0) | (be_ref[i] != be_ref[jnp.maximum(i - 1, 0)]))
    def _():
        w2b_ref[...] = w2_ref[...].astype(BF16)

    @pl.when(i < nu_ref[0])
    def _():
        tme = xs_ref.shape[0] // p
        x = _from_slab(xs_ref, tme, p).astype(BF16)
        hg = jnp.dot(x, w1g_ref[...], preferred_element_type=F32) + b1g_ref[...]
        hl = jnp.dot(x, w1l_ref[...], preferred_element_type=F32) + b1l_ref[...]
        g = jnp.minimum(hg, SWIGLU_LIMIT)
        lin = jnp.clip(hl, -SWIGLU_LIMIT, SWIGLU_LIMIT)
        act = (g * jax.nn.sigmoid(SWIGLU_ALPHA * g) * (lin + 1.0)).astype(BF16)
        y = jnp.dot(act, w2b_ref[...], preferred_element_type=F32) + b2_ref[...]
        _to_slab(y_ref, y, tme, p)

    @pl.when(i >= nu_ref[0])
    def _():
        y_ref[...] = jnp.zeros_like(y_ref)


def _experts(block_e, n_used, xs, w1g, w1l, b1g, b1l, w2_all, layer, b2, tme):
    n_exp, d, ff = w1g.shape
    p = d // LANES
    n_rows = xs.shape[0] // p
    rows_in = lambda i, be, nu: (jnp.minimum(i, nu[0] - 1), 0)
    wsel = lambda i, be, nu: (be[i], 0, 0)
    return pl.pallas_call(
        functools.partial(_expert_kernel, p=p),
        grid_spec=pltpu.PrefetchScalarGridSpec(
            num_scalar_prefetch=2,
            grid=(n_rows // tme,),
            in_specs=[
                pl.BlockSpec((tme * p, LANES), rows_in),
                pl.BlockSpec((None, d, ff), wsel),
                pl.BlockSpec((None, d, ff), wsel),
                pl.BlockSpec((None, 1, ff), wsel),
                pl.BlockSpec((None, 1, ff), wsel),
                pl.BlockSpec((None, None, ff, d), lambda i, be, nu: (layer, be[i], 0, 0)),
                pl.BlockSpec((None, 1, d), wsel),
            ],
            out_specs=pl.BlockSpec((tme * p, LANES), lambda i, be, nu: (i, 0)),
            scratch_shapes=[pltpu.VMEM((ff, d), BF16)],
        ),
        out_shape=jax.ShapeDtypeStruct((n_rows * p, LANES), F32),
        compiler_params=_params(("arbitrary",), 58),
        name="moe_experts",
    )(block_e, n_used, xs, w1g, w1l, b1g, b1l, w2_all, b2)


def _combine_kernel(dest_ref, dnext_ref, y_ref, gates_ref, x_ref, gate2_ref, gfin_ref, o_ref, buf_ref, sem,
                    *, p, final):
    i = pl.program_id(0)
    tt = x_ref.shape[0]
    slot = i % 2

    def fetch(d_ref, s):
        def body(t, c):
            for k in range(TOP_K):
                pltpu.make_async_copy(y_ref.at[pl.ds(d_ref[t * TOP_K + k] * p, p), :],
                                      buf_ref.at[s, k, pl.ds(t * p, p), :], sem.at[s]).start()
            return c
        lax.fori_loop(0, tt, body, 0)

    @pl.when(i == 0)
    def _():
        fetch(dest_ref, 0)

    @pl.when(i + 1 < pl.num_programs(0))
    def _():
        fetch(dnext_ref, 1 - slot)

    for k in range(TOP_K):
        pltpu.make_async_copy(y_ref.at[pl.ds(0, tt * p), :], buf_ref.at[slot, k], sem.at[slot]).wait()
    gates = gates_ref[...]
    acc = gates[:, 0:1] * _from_slab(buf_ref.at[slot, 0], tt, p)
    for k in range(1, TOP_K):
        acc = acc + gates[:, k:k + 1] * _from_slab(buf_ref.at[slot, k], tt, p)
    xn = x_ref[...] + gate2_ref[...] * acc
    if final:
        ms = jnp.mean(xn * xn, axis=-1, keepdims=True)
        xn = xn * lax.rsqrt(ms + EPS) * gfin_ref[...]
    o_ref[...] = xn


def _combine(dest_flat, y_slab, gates, x, gate2, g_final, seq, tt, final):
    n, d = x.shape
    p = d // LANES
    tiles = seq // tt
    n_tiles = n // tt
    return pl.pallas_call(
        functools.partial(_combine_kernel, p=p, final=final),
        grid=(n_tiles,),
        in_specs=[
            pl.BlockSpec((tt * TOP_K,), lambda i: (i,), memory_space=pltpu.SMEM),
            pl.BlockSpec((tt * TOP_K,), lambda i: (jnp.minimum(i + 1, n_tiles - 1),), memory_space=pltpu.SMEM),
            pl.BlockSpec(memory_space=pl.ANY),
            pl.BlockSpec((tt, TOP_K), lambda i: (i, 0)),
            pl.BlockSpec((tt, d), lambda i: (i, 0)),
            pl.BlockSpec((None, 1, d), lambda i: (i // tiles, 0, 0)),
            pl.BlockSpec((1, d), lambda i: (0, 0)),
        ],
        out_specs=pl.BlockSpec((tt, d), lambda i: (i, 0)),
        out_shape=jax.ShapeDtypeStruct((n, d), F32),
        scratch_shapes=[pltpu.VMEM((2, TOP_K, tt * p, LANES), F32), pltpu.SemaphoreType.DMA((2,))],
        compiler_params=_params(("arbitrary",), 40),
        name="moe_combine",
    )(dest_flat, dest_flat, y_slab, gates, x, gate2, g_final)


def _moe(x, tok_slab, logits_t, gate2, w1_all, b1, w2_all, b2, layer, g_final, seq, final):
    n, d = x.shape
    n_exp = w1_all.shape[1]
    ff = w2_all.shape[2]
    tme = 256
    top_idx, gates, rank, counts = _route(logits_t, _row_tile(n, 256))
    counts = counts[:, 0]
    padded = (counts + tme - 1) // tme * tme
    pad_end = jnp.cumsum(padded)
    pad_start = pad_end - padded
    experts = jnp.arange(n_exp, dtype=jnp.int32)[:, None, None]
    start = jnp.sum(jnp.where(top_idx[None] == experts, pad_start[:, None, None], 0), axis=0)
    dest = (start + rank).T.reshape(-1).astype(jnp.int32)
    n_rows = (n * TOP_K + tme - 1) // tme * tme + n_exp * tme
    n_blocks = n_rows // tme
    blk_start = jnp.arange(n_blocks, dtype=jnp.int32) * tme
    block_e = jnp.minimum(jnp.sum(blk_start[:, None] >= pad_end[None, :], axis=1), n_exp - 1).astype(jnp.int32)
    n_used = (pad_end[-1] // tme).astype(jnp.int32).reshape(1)
    pad_lo = (pad_start + counts).astype(jnp.int32)
    pad_n = (padded - counts).astype(jnp.int32)
    tail = jnp.concatenate([n_used, n_blocks - n_used]).astype(jnp.int32)

    xs = _dispatch(pad_lo, pad_n, tail, dest, tok_slab, n_rows, d, _row_tile(n, 256), tme)
    w1g, w1l = _w1_split(w1_all, layer)
    b1g = b1[:, 0::2].reshape(n_exp, 1, ff)
    b1l = b1[:, 1::2].reshape(n_exp, 1, ff)
    y = _experts(block_e, n_used, xs, w1g, w1l, b1g, b1l, w2_all, layer, b2.reshape(n_exp, 1, d), tme)
    return _combine(dest, y, gates.T, x, gate2, g_final.reshape(1, d), seq, _row_tile(seq, 128), final)


def kernel(x, c, ctx, c_ctx, w_mod, b_mod, g_mix, g_ffn, ab_w_in, ab_q_gain, ab_k_gain, ab_w_out,
           cv_w_in, cv_b_in, cv_w_dw, cv_b_dw, cv_ln_g, cv_ln_b, cv_w_out, cv_b_out,
           moe_w_router, moe_b_router, moe_w1, moe_b1, moe_w2, moe_b2, g_final):
    bsz, seq, d = x.shape
    ctx_len = ctx.shape[1]
    depth = w_mod.shape[0]
    n = bsz * seq
    assert bsz + 1 <= MOD_ROWS and d % LANES == 0
    tm = _row_tile(seq, 256)
    assert ctx_len % tm == 0

    cond = jnp.zeros((MOD_ROWS, d), F32).at[:bsz].set(c).at[bsz].set(c_ctx)
    mod = _modulation(cond, w_mod, b_mod).reshape(depth, MOD_ROWS, N_MOD, d)

    xf = x.reshape(n, d)
    for i in range(depth):
        j = i // 2
        sh1, sc1, gt1, sh2, sc2, gt2 = (mod[i, :, m].reshape(MOD_ROWS, 1, d) for m in range(N_MOD))
        a1 = g_mix[i] * (1.0 + sc1)
        a2 = g_ffn[i] * (1.0 + sc2)
        router = _router_operands(moe_w_router[i], moe_b_router[i])
        if i % 2 == 0:
            assert not any(k % 2 == 0 for k in range(i + 1, depth)), "context update path not needed at this depth"
            xin = jnp.concatenate([xf.reshape(bsz, seq, d), ctx], axis=1)
            cos_t, sin_t = _rope_tables(seq, ctx_len)
            q, k, v, f = _ab_in(xin, a1, sh1, ab_w_in[j].astype(BF16), ab_q_gain[j].reshape(1, -1),
                                ab_k_gain[j].reshape(1, -1), cos_t, sin_t, seq, tm)
            attn = _attention(q, k, v, seq, tm)
            fmix = _fourier_mix(f, seq)
            w_out = ab_w_out[j].astype(BF16)
            xf, tok, logits = _ab_out(attn.reshape(n, ATTN_WIDTH), fmix.reshape(n, FOURIER_WIDTH),
                                      w_out[:ATTN_WIDTH], w_out[ATTN_WIDTH:], xf, gt1, a2, sh2,
                                      router, seq, tm)
        else:
            u = _cv_in(xf, a1, sh1, cv_w_in[j].astype(BF16), cv_b_in[j].reshape(1, -1), seq, tm)
            v = _dwconv(u.reshape(bsz, seq, d), cv_w_dw[j], cv_b_dw[j].reshape(1, -1),
                        _row_tile(seq, 128), _row_tile(d, 512))
            xf, tok, logits = _cv_out(v.reshape(n, d), cv_ln_g[j].reshape(1, -1), cv_ln_b[j].reshape(1, -1),
                                      cv_w_out[j].astype(BF16), cv_b_out[j].reshape(1, -1), xf, gt1, a2, sh2,
                                      router, seq, tm)
        xf = _moe(xf, tok, logits, gt2, moe_w1, moe_b1[i], moe_w2, moe_b2[i], i, g_final, seq,
                  final=(i == depth - 1))
    return xf.reshape(bsz, seq, d)
```

```python
import functools

import numpy as np
import jax
import jax.numpy as jnp
from jax import lax
from jax.experimental import pallas as pl
from jax.experimental.pallas import tpu as pltpu

F32 = jnp.float32
BF16 = jnp.bfloat16
HIGHEST = lax.Precision.HIGHEST

LANES = 128
SUBLANES = 8
HEAD_DIM = 128
N_Q_HEADS = 8
N_KV_HEADS = 2
Q_PER_KV = N_Q_HEADS // N_KV_HEADS
ATTN_WIDTH = N_Q_HEADS * HEAD_DIM
KV_WIDTH = N_KV_HEADS * HEAD_DIM
GRID_W = 64
ROPE_AXIS_DIM = HEAD_DIM // 2
ROPE_THETA = 10000.0
FOURIER_GROUPS = 8
FOURIER_GROUP_DIM = 128
FOURIER_WIDTH = FOURIER_GROUPS * FOURIER_GROUP_DIM
TOP_K = 4
SWIGLU_LIMIT = 7.0
SWIGLU_ALPHA = 1.702
N_MOD = 6
EPS = 1e-6
MOD_ROWS = 8
HALO = 16
SUB_ROWS = 256
MIB = 1024 * 1024


def _params(sem, vmem_mib):
    return pltpu.CompilerParams(dimension_semantics=sem, vmem_limit_bytes=vmem_mib * MIB)


def _row_tile(n, want):
    t = min(n, want)
    assert n % t == 0, (n, t)
    return t


def _resident(shape, index_map):
    return pl.BlockSpec(shape, index_map, pipeline_mode=pl.Buffered(1))


def _mod_kernel(c_ref, w_ref, b_ref, o_ref):
    c = c_ref[...]
    s = c * jax.nn.sigmoid(c)
    o_ref[...] = jnp.dot(s, w_ref[...], preferred_element_type=F32, precision=HIGHEST) + b_ref[...]


def _modulation(cond, w_mod, b_mod):
    depth, d, n = w_mod.shape
    tn = _row_tile(n, min(1024, d))
    return pl.pallas_call(
        _mod_kernel,
        grid=(depth, n // tn),
        in_specs=[
            pl.BlockSpec((MOD_ROWS, d), lambda i, j: (0, 0)),
            pl.BlockSpec((None, d, tn), lambda i, j: (i, 0, j)),
            pl.BlockSpec((None, 1, tn), lambda i, j: (i, 0, j)),
        ],
        out_specs=pl.BlockSpec((None, MOD_ROWS, tn), lambda i, j: (i, 0, j)),
        out_shape=jax.ShapeDtypeStruct((depth, MOD_ROWS, n), F32),
        compiler_params=_params(("arbitrary", "arbitrary"), 40),
        name="adaln_modulation",
    )(cond, w_mod, b_mod.reshape(depth, 1, n))


def _norm_mod(x, a, b):
    ms = jnp.mean(x * x, axis=-1, keepdims=True)
    return x * lax.rsqrt(ms + EPS) * a + b


def _to_slab(ref, val, rows, p, first=0):
    for j in range(p):
        ref[pl.ds(first * p + j, rows, stride=p), :] = val[:, j * LANES:(j + 1) * LANES]


def _from_slab(ref, rows, p):
    return jnp.concatenate([ref[pl.ds(j, rows, stride=p), :] for j in range(p)], axis=1)


def _pack_bf16_pairs(v):
    half = v.shape[1] // 2
    bits = lax.bitcast_convert_type(v.astype(BF16).astype(F32), jnp.uint32)
    return bits[:, half:] | (bits[:, :half] >> 16)


def _unpack_bf16_pairs(w):
    lo = lax.bitcast_convert_type(w << 16, F32).astype(BF16)
    hi = lax.bitcast_convert_type(w & jnp.uint32(0xFFFF0000), F32).astype(BF16)
    return jnp.concatenate([lo, hi], axis=1)


def _split_bf16(v):
    hi = v.astype(BF16)
    return hi, (v - hi.astype(F32)).astype(BF16)


def _residual_epilogue(mix, r0, x_ref, gate_ref, a2_ref, b2_ref, wrh_ref, wrl_ref, br_ref, xo_ref, tok_ref, lg_ref):
    tm, d = mix.shape
    xn = x_ref[r0:r0 + tm, :] + gate_ref[...] * mix
    xo_ref[r0:r0 + tm, :] = xn
    tok = _norm_mod(xn, a2_ref[...], b2_ref[...])
    tok_hi, tok_lo = _split_bf16(tok)
    nt = (((1,), (1,)), ((), ()))
    wrh = wrh_ref[...]
    lg = (lax.dot_general(wrh, tok_hi, nt, preferred_element_type=F32)
          + lax.dot_general(wrh, tok_lo, nt, preferred_element_type=F32)
          + lax.dot_general(wrl_ref[...], tok_hi, nt, preferred_element_type=F32))
    lg_ref[:, r0:r0 + tm] = lg + br_ref[...]
    _to_slab(tok_ref, _pack_bf16_pairs(tok), tm, d // (2 * LANES), first=r0)


def _epilogue_specs(tm, d, n_exp, tiles_per_batch):
    bmap = lambda i: (i // tiles_per_batch, 0, 0)
    in_specs = [
        pl.BlockSpec((tm, d), lambda i: (i, 0)),
        pl.BlockSpec((None, 1, d), bmap),
        pl.BlockSpec((None, 1, d), bmap),
        pl.BlockSpec((None, 1, d), bmap),
        _resident((n_exp, d), lambda i: (0, 0)),
        _resident((n_exp, d), lambda i: (0, 0)),
        _resident((n_exp, 1), lambda i: (0, 0)),
    ]
    p = d // (2 * LANES)
    out_specs = [
        pl.BlockSpec((tm, d), lambda i: (i, 0)),
        pl.BlockSpec((tm * p, LANES), lambda i: (i, 0)),
        pl.BlockSpec((n_exp, tm), lambda i: (0, i)),
    ]
    return in_specs, out_specs


def _epilogue_out_shape(n, d, n_exp):
    return [
        jax.ShapeDtypeStruct((n, d), F32),
        jax.ShapeDtypeStruct((n * (d // (2 * LANES)), LANES), jnp.uint32),
        jax.ShapeDtypeStruct((n_exp, n), F32),
    ]


def _router_operands(w_router, b_router):
    wt = w_router.T
    hi = wt.astype(BF16)
    lo = (wt - hi.astype(F32)).astype(BF16)
    return hi, lo, b_router.reshape(-1, 1)


def _ab_in_kernel(x_ref, a_ref, b_ref, w_ref, qg_ref, kg_ref, cos_ref, sin_ref,
                  q_ref, k_ref, v_ref, f_ref):
    h = _norm_mod(x_ref[...], a_ref[...], b_ref[...]).astype(BF16)
    y = jnp.dot(h, w_ref[...], preferred_element_type=F32)
    cos = cos_ref[...]
    sin = sin_ref[...]
    lane = lax.broadcasted_iota(jnp.int32, cos.shape, 1)
    low = (lane % (2 * (ROPE_AXIS_DIM // 2))) < (ROPE_AXIS_DIM // 2)

    def head(col, gain, scale):
        blk = y[:, col:col + HEAD_DIM]
        ms = jnp.mean(blk * blk, axis=-1, keepdims=True)
        r = blk * lax.rsqrt(ms + EPS) * gain
        up = pltpu.roll(r, HEAD_DIM - ROPE_AXIS_DIM // 2, axis=1)
        dn = pltpu.roll(r, ROPE_AXIS_DIM // 2, axis=1)
        return (r * cos + jnp.where(low, up, dn) * sin) * scale

    for hd in range(N_Q_HEADS):
        q_ref[:, hd * HEAD_DIM:(hd + 1) * HEAD_DIM] = head(
            hd * HEAD_DIM, qg_ref[...], HEAD_DIM ** -0.5).astype(BF16)
    for hd in range(N_KV_HEADS):
        k_ref[:, hd * HEAD_DIM:(hd + 1) * HEAD_DIM] = head(
            ATTN_WIDTH + hd * HEAD_DIM, kg_ref[...], 1.0).astype(BF16)
    v_ref[...] = y[:, ATTN_WIDTH + KV_WIDTH:ATTN_WIDTH + 2 * KV_WIDTH].astype(BF16)
    f_ref[...] = y[:, ATTN_WIDTH + 2 * KV_WIDTH:].astype(BF16)


def _ab_in(xin, a_mod, b_mod, w_in, q_gain, k_gain, cos_t, sin_t, seq, tm):
    bsz, lt, d = xin.shape
    n_out = w_in.shape[1]
    lat_tiles = seq // tm
    mmap = lambda b, t: (jnp.where(t < lat_tiles, b, bsz), 0, 0)
    row = lambda b, t: (b, t, 0)
    return pl.pallas_call(
        _ab_in_kernel,
        grid=(bsz, lt // tm),
        in_specs=[
            pl.BlockSpec((None, tm, d), row),
            pl.BlockSpec((None, 1, d), mmap),
            pl.BlockSpec((None, 1, d), mmap),
            _resident((d, n_out), lambda b, t: (0, 0)),
            _resident((1, HEAD_DIM), lambda b, t: (0, 0)),
            _resident((1, HEAD_DIM), lambda b, t: (0, 0)),
            pl.BlockSpec((tm, HEAD_DIM), lambda b, t: (t, 0)),
            pl.BlockSpec((tm, HEAD_DIM), lambda b, t: (t, 0)),
        ],
        out_specs=[
            pl.BlockSpec((None, tm, ATTN_WIDTH), row),
            pl.BlockSpec((None, tm, KV_WIDTH), row),
            pl.BlockSpec((None, tm, KV_WIDTH), row),
            pl.BlockSpec((None, tm, FOURIER_WIDTH), row),
        ],
        out_shape=[
            jax.ShapeDtypeStruct((bsz, lt, ATTN_WIDTH), BF16),
            jax.ShapeDtypeStruct((bsz, lt, KV_WIDTH), BF16),
            jax.ShapeDtypeStruct((bsz, lt, KV_WIDTH), BF16),
            jax.ShapeDtypeStruct((bsz, lt, FOURIER_WIDTH), BF16),
        ],
        compiler_params=_params(("arbitrary", "arbitrary"), 48),
        name="ab_in_proj",
    )(xin, a_mod, b_mod, w_in, q_gain, k_gain, cos_t, sin_t)


def _rope_tables(seq, ctx_len):
    pos = np.arange(seq)
    inv_freq = ROPE_THETA ** (-np.arange(0, ROPE_AXIS_DIM, 2, dtype=np.float32) / ROPE_AXIS_DIM)
    ang_r = (pos // GRID_W).astype(np.float32)[:, None] * inv_freq.astype(np.float32)
    ang_c = (pos % GRID_W).astype(np.float32)[:, None] * inv_freq.astype(np.float32)
    ang_r = jnp.asarray(ang_r, F32)
    ang_c = jnp.asarray(ang_c, F32)
    cos = jnp.concatenate([jnp.cos(ang_r), jnp.cos(ang_r), jnp.cos(ang_c), jnp.cos(ang_c)], axis=1)
    sin = jnp.concatenate([-jnp.sin(ang_r), jnp.sin(ang_r), -jnp.sin(ang_c), jnp.sin(ang_c)], axis=1)
    cos = jnp.concatenate([cos, jnp.ones((ctx_len, HEAD_DIM), F32)], axis=0)
    sin = jnp.concatenate([sin, jnp.zeros((ctx_len, HEAD_DIM), F32)], axis=0)
    return cos, sin


def _attn_kernel(q_ref, k_ref, v_ref, o_ref):
    k = k_ref[...]
    v = v_ref[...]
    for g in range(Q_PER_KV):
        q = q_ref[:, g * HEAD_DIM:(g + 1) * HEAD_DIM]
        s = lax.dot_general(q, k, (((1,), (1,)), ((), ())), preferred_element_type=F32)
        m = jnp.max(s, axis=-1, keepdims=True)
        p = jnp.exp(s - m)
        l = jnp.sum(p, axis=-1, keepdims=True)
        o = jnp.dot(p.astype(BF16), v, preferred_element_type=F32)
        o_ref[:, g * HEAD_DIM:(g + 1) * HEAD_DIM] = (o / l).astype(BF16)


def _attention(q, k, v, seq, tq):
    bsz, lt, _ = q.shape
    gw = Q_PER_KV * HEAD_DIM
    return pl.pallas_call(
        _attn_kernel,
        grid=(bsz, N_KV_HEADS, seq // tq),
        in_specs=[
            pl.BlockSpec((None, tq, gw), lambda b, h, i: (b, i, h)),
            pl.BlockSpec((None, lt, HEAD_DIM), lambda b, h, i: (b, 0, h)),
            pl.BlockSpec((None, lt, HEAD_DIM), lambda b, h, i: (b, 0, h)),
        ],
        out_specs=pl.BlockSpec((None, tq, gw), lambda b, h, i: (b, i, h)),
        out_shape=jax.ShapeDtypeStruct((bsz, seq, ATTN_WIDTH), BF16),
        compiler_params=_params(("arbitrary", "arbitrary", "arbitrary"), 48),
        name="attention",
    )(q, k, v)


def _fourier_chan_kernel(f_ref, cr_ref, ci_ref, zr_ref, zi_ref):
    for g in range(FOURIER_GROUPS):
        cols = slice(g * FOURIER_GROUP_DIM, (g + 1) * FOURIER_GROUP_DIM)
        u = f_ref[:, cols]
        zr_ref[:, cols] = jnp.dot(u, cr_ref[...], preferred_element_type=F32)
        zi_ref[:, cols] = jnp.dot(u, ci_ref[...], preferred_element_type=F32)


def _fourier_outer_kernel(zr_ref, zi_ref, fr_ref, fi_ref, yr_ref, yi_ref):
    fr = fr_ref[...]
    fi = fi_ref[...]
    for jj in range(zr_ref.shape[1]):
        zr = zr_ref[:, jj, :].astype(BF16)
        zi = zi_ref[:, jj, :].astype(BF16)
        yr_ref[:, jj, :] = (jnp.dot(fr, zr, preferred_element_type=F32)
                            - jnp.dot(fi, zi, preferred_element_type=F32))
        yi_ref[:, jj, :] = (jnp.dot(fr, zi, preferred_element_type=F32)
                            + jnp.dot(fi, zr, preferred_element_type=F32))


def _fourier_inner_kernel(yr_ref, yi_ref, twr_ref, twi_ref, gr_ref, gi_ref, o_ref, *, scale):
    gr = gr_ref[...]
    gi = gi_ref[...]
    for kk in range(yr_ref.shape[0]):
        yr = yr_ref[kk]
        yi = yi_ref[kk]
        twr = twr_ref[kk]
        twi = twi_ref[kk]
        tr = (twr * yr - twi * yi).astype(BF16)
        ti = (twr * yi + twi * yr).astype(BF16)
        x = jnp.dot(gr, tr, preferred_element_type=F32) - jnp.dot(gi, ti, preferred_element_type=F32)
        o_ref[:, kk, :] = x * scale


def _dft_parts(n):
    jk = np.outer(np.arange(n), np.arange(n)) % n
    ang = 2 * np.pi * jk / n
    return np.cos(ang), -np.sin(ang)


def _fourier_mix(f, seq):
    bsz = f.shape[0]
    gd = FOURIER_GROUP_DIM
    width = FOURIER_WIDTH
    cr, ci = _dft_parts(gd)
    tm = _row_tile(seq, 512)
    zr, zi = pl.pallas_call(
        _fourier_chan_kernel,
        grid=(bsz, seq // tm),
        in_specs=[
            pl.BlockSpec((None, tm, width), lambda b, i: (b, i, 0)),
            _resident((gd, gd), lambda b, i: (0, 0)),
            _resident((gd, gd), lambda b, i: (0, 0)),
        ],
        out_specs=[pl.BlockSpec((None, tm, width), lambda b, i: (b, i, 0))] * 2,
        out_shape=[jax.ShapeDtypeStruct((bsz, seq, width), F32)] * 2,
        compiler_params=_params(("arbitrary", "arbitrary"), 32),
        name="fourier_channels",
    )(f, jnp.asarray(cr, BF16), jnp.asarray(ci, BF16))

    n_b = min(seq, 64)
    n_a = seq // n_b
    g2 = min(n_b, SUBLANES)
    fr, fi = _dft_parts(n_a)
    split = lambda b, j: (b, 0, j, 0)
    yr, yi = pl.pallas_call(
        _fourier_outer_kernel,
        grid=(bsz, n_b // g2),
        in_specs=[
            pl.BlockSpec((None, n_a, g2, width), split),
            pl.BlockSpec((None, n_a, g2, width), split),
            _resident((n_a, n_a), lambda b, j: (0, 0)),
            _resident((n_a, n_a), lambda b, j: (0, 0)),
        ],
        out_specs=[pl.BlockSpec((None, n_a, g2, width), split)] * 2,
        out_shape=[jax.ShapeDtypeStruct((bsz, n_a, n_b, width), F32)] * 2,
        compiler_params=_params(("arbitrary", "arbitrary"), 40),
        name="fourier_outer_dft",
    )(zr.reshape(bsz, n_a, n_b, width), zi.reshape(bsz, n_a, n_b, width),
      jnp.asarray(fr, BF16), jnp.asarray(fi, BF16))

    ang = 2 * np.pi * np.outer(np.arange(n_a), np.arange(n_b)) / seq
    twr = jnp.asarray(np.cos(ang)[:, :, None], F32)
    twi = jnp.asarray(-np.sin(ang)[:, :, None], F32)
    gr, gi = _dft_parts(n_b)
    g1 = min(n_a, 16)
    out = pl.pallas_call(
        functools.partial(_fourier_inner_kernel, scale=float((seq * gd) ** -0.5)),
        grid=(bsz, n_a // g1),
        in_specs=[
            pl.BlockSpec((None, g1, n_b, width), lambda b, j: (b, j, 0, 0)),
            pl.BlockSpec((None, g1, n_b, width), lambda b, j: (b, j, 0, 0)),
            pl.BlockSpec((g1, n_b, 1), lambda b, j: (j, 0, 0)),
            pl.BlockSpec((g1, n_b, 1), lambda b, j: (j, 0, 0)),
            _resident((n_b, n_b), lambda b, j: (0, 0)),
            _resident((n_b, n_b), lambda b, j: (0, 0)),
        ],
        out_specs=pl.BlockSpec((None, n_b, g1, width), lambda b, j: (b, 0, j, 0)),
        out_shape=jax.ShapeDtypeStruct((bsz, n_b, n_a, width), F32),
        compiler_params=_params(("arbitrary", "arbitrary"), 48),
        name="fourier_inner_dft",
    )(yr, yi, twr, twi, jnp.asarray(gr, BF16), jnp.asarray(gi, BF16))
    return out.reshape(bsz, seq, width)


def _ab_out_kernel(at_ref, fm_ref, wa_ref, wf_ref, x_ref, gate_ref, a2_ref, b2_ref, wrh_ref, wrl_ref, br_ref,
                   xo_ref, tok_ref, lg_ref):
    for r0 in range(0, x_ref.shape[0], SUB_ROWS):
        rows = slice(r0, r0 + SUB_ROWS)
        mix = (jnp.dot(at_ref[rows, :], wa_ref[...], preferred_element_type=F32)
               + jnp.dot(fm_ref[rows, :].astype(BF16), wf_ref[...], preferred_element_type=F32))
        _residual_epilogue(mix, r0, x_ref, gate_ref, a2_ref, b2_ref, wrh_ref, wrl_ref, br_ref,
                           xo_ref, tok_ref, lg_ref)


def _ab_out(attn, fmix, w_attn, w_four, x, gate, a2, b2, router, seq, tm):
    n, d = x.shape
    n_exp = router[0].shape[0]
    ep_in, ep_out = _epilogue_specs(tm, d, n_exp, seq // tm)
    return pl.pallas_call(
        _ab_out_kernel,
        grid=(n // tm,),
        in_specs=[
            pl.BlockSpec((tm, ATTN_WIDTH), lambda i: (i, 0)),
            pl.BlockSpec((tm, FOURIER_WIDTH), lambda i: (i, 0)),
            _resident((ATTN_WIDTH, d), lambda i: (0, 0)),
            _resident((FOURIER_WIDTH, d), lambda i: (0, 0)),
        ] + ep_in,
        out_specs=ep_out,
        out_shape=_epilogue_out_shape(n, d, n_exp),
        compiler_params=_params(("arbitrary",), 56),
        name="ab_out_proj",
    )(attn, fmix, w_attn, w_four, x, gate, a2, b2, *router)


def _cv_in_kernel(x_ref, a_ref, b_ref, w_ref, bias_ref, u_ref):
    d = x_ref.shape[1]
    h = _norm_mod(x_ref[...], a_ref[...], b_ref[...]).astype(BF16)
    y = jnp.dot(h, w_ref[...], preferred_element_type=F32) + bias_ref[...]
    u_ref[...] = y[:, :d] * jax.nn.sigmoid(y[:, d:])


def _cv_in(x, a_mod, b_mod, w_in, b_in, seq, tm):
    n, d = x.shape
    tiles = seq // tm
    bmap = lambda i: (i // tiles, 0, 0)
    return pl.pallas_call(
        _cv_in_kernel,
        grid=(n // tm,),
        in_specs=[
            pl.BlockSpec((tm, d), lambda i: (i, 0)),
            pl.BlockSpec((None, 1, d), bmap),
            pl.BlockSpec((None, 1, d), bmap),
            _resident((d, 2 * d), lambda i: (0, 0)),
            _resident((1, 2 * d), lambda i: (0, 0)),
        ],
        out_specs=pl.BlockSpec((tm, d), lambda i: (i, 0)),
        out_shape=jax.ShapeDtypeStruct((n, d), F32),
        compiler_params=_params(("arbitrary",), 48),
        name="conv_in_proj_glu",
    )(x, a_mod, b_mod, w_in, b_in)


def _dwconv_kernel(prev_ref, cur_ref, next_ref, w_ref, b_ref, o_ref, win_ref, *, window, row_chunk):
    i = pl.program_id(1)
    tm, tc = cur_ref.shape
    first = i == 0
    last = i == pl.num_programs(1) - 1
    win_ref[0:HALO, :] = jnp.where(first, 0.0, prev_ref[...])
    win_ref[HALO:HALO + tm, :] = cur_ref[...]
    win_ref[HALO + tm:, :] = jnp.where(last, 0.0, next_ref[...])
    lead = HALO - window // 2
    ext = row_chunk + SUBLANES
    assert tm % row_chunk == 0 and lead + window - 1 < 4 * SUBLANES and ext + 3 * SUBLANES <= row_chunk + 2 * HALO
    for c in range(tc // LANES):
        cols = slice(c * LANES, (c + 1) * LANES)
        for r in range(tm // row_chunk):
            r0 = r * row_chunk
            acc = jnp.zeros((row_chunk, LANES), F32) + b_ref[:, cols]
            for b in range(SUBLANES):
                part = None
                for a in range(4):
                    j = SUBLANES * a + b - lead
                    if 0 <= j < window:
                        lo = r0 + SUBLANES * a
                        term = w_ref[j:j + 1, cols] * win_ref[lo:lo + ext, cols]
                        part = term if part is None else part + term
                acc = acc + part[b:b + row_chunk]
            o_ref[r0:r0 + row_chunk, cols] = acc


def _dwconv(u, w_dw, b_dw, tm, tc):
    bsz, seq, d = u.shape
    window = w_dw.shape[0]
    assert window // 2 <= HALO and tm % HALO == 0
    hb = tm // HALO
    n_halo = seq // HALO
    return pl.pallas_call(
        functools.partial(_dwconv_kernel, window=window, row_chunk=min(tm, 64)),
        grid=(bsz, seq // tm, d // tc),
        in_specs=[
            pl.BlockSpec((None, HALO, tc), lambda b, i, c: (b, jnp.maximum(i * hb - 1, 0), c)),
            pl.BlockSpec((None, tm, tc), lambda b, i, c: (b, i, c)),
            pl.BlockSpec((None, HALO, tc), lambda b, i, c: (b, jnp.minimum((i + 1) * hb, n_halo - 1), c)),
            pl.BlockSpec((window, tc), lambda b, i, c: (0, c)),
            pl.BlockSpec((1, tc), lambda b, i, c: (0, c)),
        ],
        out_specs=pl.BlockSpec((None, tm, tc), lambda b, i, c: (b, i, c)),
        out_shape=jax.ShapeDtypeStruct((bsz, seq, d), F32),
        scratch_shapes=[pltpu.VMEM((tm + 2 * HALO, tc), F32)],
        compiler_params=_params(("arbitrary", "arbitrary", "arbitrary"), 32),
        name="depthwise_conv",
    )(u, u, u, w_dw, b_dw)


def _cv_out_kernel(v_ref, lng_ref, lnb_ref, w_ref, bo_ref, x_ref, gate_ref, a2_ref, b2_ref, wrh_ref, wrl_ref, br_ref,
                   xo_ref, tok_ref, lg_ref):
    for r0 in range(0, x_ref.shape[0], SUB_ROWS):
        v = v_ref[r0:r0 + SUB_ROWS, :]
        mu = jnp.mean(v, axis=-1, keepdims=True)
        vc = v - mu
        var = jnp.mean(vc * vc, axis=-1, keepdims=True)
        y = vc * lax.rsqrt(var + EPS) * lng_ref[...] + lnb_ref[...]
        y = (y * jax.nn.sigmoid(y)).astype(BF16)
        mix = jnp.dot(y, w_ref[...], preferred_element_type=F32) + bo_ref[...]
        _residual_epilogue(mix, r0, x_ref, gate_ref, a2_ref, b2_ref, wrh_ref, wrl_ref, br_ref,
                           xo_ref, tok_ref, lg_ref)


def _cv_out(v, ln_g, ln_b, w_out, b_out, x, gate, a2, b2, router, seq, tm):
    n, d = x.shape
    n_exp = router[0].shape[0]
    ep_in, ep_out = _epilogue_specs(tm, d, n_exp, seq // tm)
    return pl.pallas_call(
        _cv_out_kernel,
        grid=(n // tm,),
        in_specs=[
            pl.BlockSpec((tm, d), lambda i: (i, 0)),
            _resident((1, d), lambda i: (0, 0)),
            _resident((1, d), lambda i: (0, 0)),
            _resident((d, d), lambda i: (0, 0)),
            _resident((1, d), lambda i: (0, 0)),
        ] + ep_in,
        out_specs=ep_out,
        out_shape=_epilogue_out_shape(n, d, n_exp),
        compiler_params=_params(("arbitrary",), 56),
        name="conv_out_proj",
    )(v, ln_g, ln_b, w_out, b_out, x, gate, a2, b2, *router)


def _route_kernel(lg_ref, idx_ref, gate_ref, rank_ref, cnt_ref, run_ref):
    i = pl.program_id(0)

    @pl.when(i == 0)
    def _():
        run_ref[...] = jnp.zeros_like(run_ref)

    lg = lg_ref[...]
    n_exp, tr = lg.shape
    sub = lax.broadcasted_iota(jnp.int32, (n_exp, tr), 0).astype(F32)
    work = lg
    vals, idxs = [], []
    member = jnp.zeros((n_exp, tr), F32)
    for _ in range(TOP_K):
        m = jnp.max(work, axis=0, keepdims=True)
        sel = jnp.min(jnp.where(work == m, sub, float(n_exp)), axis=0, keepdims=True)
        hit = sub == sel
        vals.append(m)
        idxs.append(sel)
        member = jnp.where(hit, 1.0, member)
        work = jnp.where(hit, -jnp.inf, work)
    es = [jnp.exp(v - vals[0]) for v in vals]
    tot = es[0] + es[1] + es[2] + es[3]
    r_i = lax.broadcasted_iota(jnp.int32, (tr, tr), 0)
    c_i = lax.broadcasted_iota(jnp.int32, (tr, tr), 1)
    earlier = (r_i < c_i).astype(BF16)
    before = jnp.dot(member.astype(BF16), earlier, preferred_element_type=F32) + run_ref[...]
    for k in range(TOP_K):
        idx_ref[k:k + 1, :] = idxs[k].astype(jnp.int32)
        gate_ref[k:k + 1, :] = es[k] / tot
        rank_ref[k:k + 1, :] = jnp.sum(jnp.where(sub == idxs[k], before, 0.0),
                                       axis=0, keepdims=True).astype(jnp.int32)
    run_ref[...] += jnp.sum(member, axis=1, keepdims=True)
    cnt_ref[...] = run_ref[...].astype(jnp.int32)


def _route(logits_t, tr):
    n_exp, n = logits_t.shape
    tok = lambda i: (0, i)
    return pl.pallas_call(
        _route_kernel,
        grid=(n // tr,),
        in_specs=[pl.BlockSpec((n_exp, tr), tok)],
        out_specs=[
            pl.BlockSpec((TOP_K, tr), tok),
            pl.BlockSpec((TOP_K, tr), tok),
            pl.BlockSpec((TOP_K, tr), tok),
            pl.BlockSpec((n_exp, 1), lambda i: (0, 0)),
        ],
        out_shape=[
            jax.ShapeDtypeStruct((TOP_K, n), jnp.int32),
            jax.ShapeDtypeStruct((TOP_K, n), F32),
            jax.ShapeDtypeStruct((TOP_K, n), jnp.int32),
            jax.ShapeDtypeStruct((n_exp, 1), jnp.int32),
        ],
        scratch_shapes=[pltpu.VMEM((n_exp, 1), F32)],
        compiler_params=_params(("arbitrary",), 32),
        name="moe_route",
    )(logits_t)


def _w1_split_kernel(w_ref, perm_ref, g_ref, l_ref):
    wide = 2 * LANES
    for c in range(w_ref.shape[1] // wide):
        t = jnp.dot(w_ref[:, c * wide:(c + 1) * wide].astype(BF16), perm_ref[...], preferred_element_type=F32)
        g_ref[:, c * LANES:(c + 1) * LANES] = t[:, :LANES].astype(BF16)
        l_ref[:, c * LANES:(c + 1) * LANES] = t[:, LANES:].astype(BF16)


def _w1_split(w1_all, layer):
    _, n_exp, d, ff2 = w1_all.shape
    ff = ff2 // 2
    assert ff % LANES == 0
    perm = np.zeros((2 * LANES, 2 * LANES), np.float32)
    perm[2 * np.arange(LANES), np.arange(LANES)] = 1.0
    perm[2 * np.arange(LANES) + 1, LANES + np.arange(LANES)] = 1.0
    tk = _row_tile(d, 512)
    return pl.pallas_call(
        _w1_split_kernel,
        grid=(n_exp, d // tk),
        in_specs=[
            pl.BlockSpec((None, None, tk, ff2), lambda e, r: (layer, e, r, 0)),
            _resident((2 * LANES, 2 * LANES), lambda e, r: (0, 0)),
        ],
        out_specs=[pl.BlockSpec((None, tk, ff), lambda e, r: (e, r, 0))] * 2,
        out_shape=[jax.ShapeDtypeStruct((n_exp, d, ff), BF16)] * 2,
        compiler_params=_params(("arbitrary", "arbitrary"), 32),
        name="moe_w1_split",
    )(w1_all, jnp.asarray(perm, BF16))


def _dispatch_kernel(pad_lo_ref, pad_n_ref, tail_ref, dest_ref, tok_ref, xs_ref, zero_ref, sem, zsem, *, p, n_exp):
    i = pl.program_id(0)
    n_assign = dest_ref.shape[0]
    blk = zero_ref.shape[0]

    def row_copy(a):
        t = a // TOP_K
        return pltpu.make_async_copy(tok_ref.at[pl.ds(t * p, p), :],
                                     xs_ref.at[pl.ds(dest_ref[a] * p, p), :], sem)

    def zero_row(row):
        return pltpu.make_async_copy(zero_ref.at[pl.ds(0, p), :], xs_ref.at[pl.ds(row * p, p), :], zsem)

    def zero_block(b):
        return pltpu.make_async_copy(zero_ref, xs_ref.at[pl.ds(b * blk, blk), :], zsem)

    lax.fori_loop(0, n_assign, lambda a, c: (row_copy(a).start(), c)[1], 0)

    @pl.when(i == 0)
    def _():
        zero_ref[...] = jnp.zeros_like(zero_ref)
        for e in range(n_exp):
            lo = pad_lo_ref[e]
            lax.fori_loop(0, pad_n_ref[e], lambda j, c: (zero_row(lo + j).start(), c)[1], 0)
        first = tail_ref[0]
        lax.fori_loop(0, tail_ref[1], lambda j, c: (zero_block(first + j).start(), c)[1], 0)
        for e in range(n_exp):
            lo = pad_lo_ref[e]
            lax.fori_loop(0, pad_n_ref[e], lambda j, c: (zero_row(lo + j).wait(), c)[1], 0)
        lax.fori_loop(0, tail_ref[1], lambda j, c: (zero_block(first + j).wait(), c)[1], 0)

    for _ in range(TOP_K):
        pltpu.make_async_copy(tok_ref, xs_ref.at[pl.ds(0, tok_ref.shape[0]), :], sem).wait()


def _dispatch(pad_lo, pad_n, tail, dest_flat, tok_slab, n_rows, n, tt, tme):
    p = tok_slab.shape[0] // n
    n_exp = pad_lo.shape[0]
    return pl.pallas_call(
        functools.partial(_dispatch_kernel, p=p, n_exp=n_exp),
        grid_spec=pltpu.PrefetchScalarGridSpec(
            num_scalar_prefetch=3,
            grid=(n // tt,),
            in_specs=[
                pl.BlockSpec((tt * TOP_K,), lambda i, lo, pn, tl: (i,), memory_space=pltpu.SMEM),
                pl.BlockSpec((tt * p, LANES), lambda i, lo, pn, tl: (i, 0)),
            ],
            out_specs=pl.BlockSpec(memory_space=pl.ANY),
            scratch_shapes=[pltpu.VMEM((tme * p, LANES), tok_slab.dtype), pltpu.SemaphoreType.DMA,
                            pltpu.SemaphoreType.DMA],
        ),
        out_shape=jax.ShapeDtypeStruct((n_rows * p, LANES), tok_slab.dtype),
        compiler_params=_params(("arbitrary",), 32),
        name="moe_dispatch",
    )(pad_lo, pad_n, tail, dest_flat, tok_slab)


def _expert_kernel(be_ref, nu_ref, xs_ref, w1g_ref, w1l_ref, b1g_ref, b1l_ref, w2_ref, b2_ref, y_ref, w2b_ref, *, p):
    px = p // 2
    i = pl.program_id(0)

    @pl.when((i == 0) | (be_ref[i] != be_ref[jnp.maximum(i - 1, 0)]))
    def _():
        w2b_ref[...] = w2_ref[...].astype(BF16)

    @pl.when(i < nu_ref[0])
    def _():
        tme = xs_ref.shape[0] // px
        x = _unpack_bf16_pairs(_from_slab(xs_ref, tme, px))
        hg = jnp.dot(x, w1g_ref[...], preferred_element_type=F32) + b1g_ref[...]
        hl = jnp.dot(x, w1l_ref[...], preferred_element_type=F32) + b1l_ref[...]
        g = jnp.minimum(hg, SWIGLU_LIMIT)
        lin = jnp.clip(hl, -SWIGLU_LIMIT, SWIGLU_LIMIT)
        act = (g * jax.nn.sigmoid(SWIGLU_ALPHA * g) * (lin + 1.0)).astype(BF16)
        y = jnp.dot(act, w2b_ref[...], preferred_element_type=F32) + b2_ref[...]
        _to_slab(y_ref, y, tme, p)

    @pl.when(i >= nu_ref[0])
    def _():
        y_ref[...] = jnp.zeros_like(y_ref)


def _experts(block_e, n_used, xs, w1g, w1l, b1g, b1l, w2_all, layer, b2, tme):
    n_exp, d, ff = w1g.shape
    p = d // LANES
    px = p // 2
    n_rows = xs.shape[0] // px
    rows_in = lambda i, be, nu: (jnp.minimum(i, nu[0] - 1), 0)
    wsel = lambda i, be, nu: (be[i], 0, 0)
    return pl.pallas_call(
        functools.partial(_expert_kernel, p=p),
        grid_spec=pltpu.PrefetchScalarGridSpec(
            num_scalar_prefetch=2,
            grid=(n_rows // tme,),
            in_specs=[
                pl.BlockSpec((tme * px, LANES), rows_in),
                pl.BlockSpec((None, d, ff), wsel),
                pl.BlockSpec((None, d, ff), wsel),
                pl.BlockSpec((None, 1, ff), wsel),
                pl.BlockSpec((None, 1, ff), wsel),
                pl.BlockSpec((None, None, ff, d), lambda i, be, nu: (layer, be[i], 0, 0)),
                pl.BlockSpec((None, 1, d), wsel),
            ],
            out_specs=pl.BlockSpec((tme * p, LANES), lambda i, be, nu: (i, 0)),
            scratch_shapes=[pltpu.VMEM((ff, d), BF16)],
        ),
        out_shape=jax.ShapeDtypeStruct((n_rows * p, LANES), F32),
        compiler_params=_params(("arbitrary",), 58),
        name="moe_experts",
    )(block_e, n_used, xs, w1g, w1l, b1g, b1l, w2_all, b2)


def _combine_kernel(dest_ref, dnext_ref, y_ref, gates_ref, x_ref, gate2_ref, gfin_ref, o_ref, buf_ref, sem,
                    *, p, final):
    i = pl.program_id(0)
    tt = x_ref.shape[0]
    slot = i % 2

    def fetch(d_ref, s):
        def body(t, c):
            for k in range(TOP_K):
                pltpu.make_async_copy(y_ref.at[pl.ds(d_ref[t * TOP_K + k] * p, p), :],
                                      buf_ref.at[s, k, pl.ds(t * p, p), :], sem.at[s]).start()
            return c
        lax.fori_loop(0, tt, body, 0)

    @pl.when(i == 0)
    def _():
        fetch(dest_ref, 0)

    @pl.when(i + 1 < pl.num_programs(0))
    def _():
        fetch(dnext_ref, 1 - slot)

    for k in range(TOP_K):
        pltpu.make_async_copy(y_ref.at[pl.ds(0, tt * p), :], buf_ref.at[slot, k], sem.at[slot]).wait()
    gates = gates_ref[...]
    acc = gates[:, 0:1] * _from_slab(buf_ref.at[slot, 0], tt, p)
    for k in range(1, TOP_K):
        acc = acc + gates[:, k:k + 1] * _from_slab(buf_ref.at[slot, k], tt, p)
    xn = x_ref[...] + gate2_ref[...] * acc
    if final:
        ms = jnp.mean(xn * xn, axis=-1, keepdims=True)
        xn = xn * lax.rsqrt(ms + EPS) * gfin_ref[...]
    o_ref[...] = xn


def _combine(dest_flat, y_slab, gates, x, gate2, g_final, seq, tt, final):
    n, d = x.shape
    p = d // LANES
    tiles = seq // tt
    n_tiles = n // tt
    return pl.pallas_call(
        functools.partial(_combine_kernel, p=p, final=final),
        grid=(n_tiles,),
        in_specs=[
            pl.BlockSpec((tt * TOP_K,), lambda i: (i,), memory_space=pltpu.SMEM),
            pl.BlockSpec((tt * TOP_K,), lambda i: (jnp.minimum(i + 1, n_tiles - 1),), memory_space=pltpu.SMEM),
            pl.BlockSpec(memory_space=pl.ANY),
            pl.BlockSpec((tt, TOP_K), lambda i: (i, 0)),
            pl.BlockSpec((tt, d), lambda i: (i, 0)),
            pl.BlockSpec((None, 1, d), lambda i: (i // tiles, 0, 0)),
            pl.BlockSpec((1, d), lambda i: (0, 0)),
        ],
        out_specs=pl.BlockSpec((tt, d), lambda i: (i, 0)),
        out_shape=jax.ShapeDtypeStruct((n, d), F32),
        scratch_shapes=[pltpu.VMEM((2, TOP_K, tt * p, LANES), F32), pltpu.SemaphoreType.DMA((2,))],
        compiler_params=_params(("arbitrary",), 40),
        name="moe_combine",
    )(dest_flat, dest_flat, y_slab, gates, x, gate2, g_final)


def _moe(x, tok_slab, logits_t, gate2, w1_all, b1, w2_all, b2, layer, g_final, seq, final):
    n, d = x.shape
    n_exp = w1_all.shape[1]
    ff = w2_all.shape[2]
    tme = 256
    top_idx, gates, rank, counts = _route(logits_t, _row_tile(n, 256))
    counts = counts[:, 0]
    padded = (counts + tme - 1) // tme * tme
    pad_end = jnp.cumsum(padded)
    pad_start = pad_end - padded
    experts = jnp.arange(n_exp, dtype=jnp.int32)[:, None, None]
    start = jnp.sum(jnp.where(top_idx[None] == experts, pad_start[:, None, None], 0), axis=0)
    dest = (start + rank).T.reshape(-1).astype(jnp.int32)
    n_rows = (n * TOP_K + tme - 1) // tme * tme + n_exp * tme
    n_blocks = n_rows // tme
    blk_start = jnp.arange(n_blocks, dtype=jnp.int32) * tme
    block_e = jnp.minimum(jnp.sum(blk_start[:, None] >= pad_end[None, :], axis=1), n_exp - 1).astype(jnp.int32)
    n_used = (pad_end[-1] // tme).astype(jnp.int32).reshape(1)
    pad_lo = (pad_start + counts).astype(jnp.int32)
    pad_n = (padded - counts).astype(jnp.int32)
    tail = jnp.concatenate([n_used, n_blocks - n_used]).astype(jnp.int32)

    xs = _dispatch(pad_lo, pad_n, tail, dest, tok_slab, n_rows, n, _row_tile(n, 256), tme)
    w1g, w1l = _w1_split(w1_all, layer)
    b1g = b1[:, 0::2].reshape(n_exp, 1, ff)
    b1l = b1[:, 1::2].reshape(n_exp, 1, ff)
    y = _experts(block_e, n_used, xs, w1g, w1l, b1g, b1l, w2_all, layer, b2.reshape(n_exp, 1, d), tme)
    return _combine(dest, y, gates.T, x, gate2, g_final.reshape(1, d), seq, _row_tile(seq, 128), final)


def kernel(x, c, ctx, c_ctx, w_mod, b_mod, g_mix, g_ffn, ab_w_in, ab_q_gain, ab_k_gain, ab_w_out,
           cv_w_in, cv_b_in, cv_w_dw, cv_b_dw, cv_ln_g, cv_ln_b, cv_w_out, cv_b_out,
           moe_w_router, moe_b_router, moe_w1, moe_b1, moe_w2, moe_b2, g_final):
    bsz, seq, d = x.shape
    ctx_len = ctx.shape[1]
    depth = w_mod.shape[0]
    n = bsz * seq
    assert bsz + 1 <= MOD_ROWS and d % LANES == 0
    tm = _row_tile(seq, 256)
    tm_out = _row_tile(seq, 2 * SUB_ROWS)
    assert ctx_len % tm == 0

    cond = jnp.zeros((MOD_ROWS, d), F32).at[:bsz].set(c).at[bsz].set(c_ctx)
    mod = _modulation(cond, w_mod, b_mod).reshape(depth, MOD_ROWS, N_MOD, d)

    xf = x.reshape(n, d)
    for i in range(depth):
        j = i // 2
        sh1, sc1, gt1, sh2, sc2, gt2 = (mod[i, :, m].reshape(MOD_ROWS, 1, d) for m in range(N_MOD))
        a1 = g_mix[i] * (1.0 + sc1)
        a2 = g_ffn[i] * (1.0 + sc2)
        router = _router_operands(moe_w_router[i], moe_b_router[i])
        if i % 2 == 0:
            assert not any(k % 2 == 0 for k in range(i + 1, depth)), "context update path not needed at this depth"
            xin = jnp.concatenate([xf.reshape(bsz, seq, d), ctx], axis=1)
            cos_t, sin_t = _rope_tables(seq, ctx_len)
            q, k, v, f = _ab_in(xin, a1, sh1, ab_w_in[j].astype(BF16), ab_q_gain[j].reshape(1, -1),
                                ab_k_gain[j].reshape(1, -1), cos_t, sin_t, seq, tm)
            attn = _attention(q, k, v, seq, tm)
            fmix = _fourier_mix(f, seq)
            w_out = ab_w_out[j].astype(BF16)
            xf, tok, logits = _ab_out(attn.reshape(n, ATTN_WIDTH), fmix.reshape(n, FOURIER_WIDTH),
                                      w_out[:ATTN_WIDTH], w_out[ATTN_WIDTH:], xf, gt1, a2, sh2,
                                      router, seq, tm_out)
        else:
            u = _cv_in(xf, a1, sh1, cv_w_in[j].astype(BF16), cv_b_in[j].reshape(1, -1), seq, tm)
            v = _dwconv(u.reshape(bsz, seq, d), cv_w_dw[j], cv_b_dw[j].reshape(1, -1),
                        _row_tile(seq, 128), _row_tile(d, 512))
            xf, tok, logits = _cv_out(v.reshape(n, d), cv_ln_g[j].reshape(1, -1), cv_ln_b[j].reshape(1, -1),
                                      cv_w_out[j].astype(BF16), cv_b_out[j].reshape(1, -1), xf, gt1, a2, sh2,
                                      router, seq, tm_out)
        xf = _moe(xf, tok, logits, gt2, moe_w1, moe_b1[i], moe_w2, moe_b2[i], i, g_final, seq,
                  final=(i == depth - 1))
    return xf.reshape(bsz, seq, d)
```

```python
import functools

import numpy as np
import jax
import jax.numpy as jnp
from jax import lax
from jax.experimental import pallas as pl
from jax.experimental.pallas import tpu as pltpu

F32 = jnp.float32
BF16 = jnp.bfloat16
HIGHEST = lax.Precision.HIGHEST

LANES = 128
SUBLANES = 8
HEAD_DIM = 128
N_Q_HEADS = 8
N_KV_HEADS = 2
Q_PER_KV = N_Q_HEADS // N_KV_HEADS
ATTN_WIDTH = N_Q_HEADS * HEAD_DIM
KV_WIDTH = N_KV_HEADS * HEAD_DIM
GRID_W = 64
ROPE_AXIS_DIM = HEAD_DIM // 2
ROPE_THETA = 10000.0
FOURIER_GROUPS = 8
FOURIER_GROUP_DIM = 128
FOURIER_WIDTH = FOURIER_GROUPS * FOURIER_GROUP_DIM
TOP_K = 4
SWIGLU_LIMIT = 7.0
SWIGLU_ALPHA = 1.702
N_MOD = 6
EPS = 1e-6
MOD_ROWS = 8
HALO = 16
SUB_ROWS = 256
MIB = 1024 * 1024


def _params(sem, vmem_mib):
    return pltpu.CompilerParams(dimension_semantics=sem, vmem_limit_bytes=vmem_mib * MIB)


def _row_tile(n, want):
    t = min(n, want)
    assert n % t == 0, (n, t)
    return t


def _resident(shape, index_map):
    return pl.BlockSpec(shape, index_map, pipeline_mode=pl.Buffered(1))


def _mod_kernel(c_ref, w_ref, b_ref, o_ref):
    c = c_ref[...]
    s = c * jax.nn.sigmoid(c)
    o_ref[...] = jnp.dot(s, w_ref[...], preferred_element_type=F32, precision=HIGHEST) + b_ref[...]


def _modulation(cond, w_mod, b_mod):
    depth, d, n = w_mod.shape
    tn = _row_tile(n, min(1024, d))
    return pl.pallas_call(
        _mod_kernel,
        grid=(depth, n // tn),
        in_specs=[
            pl.BlockSpec((MOD_ROWS, d), lambda i, j: (0, 0)),
            pl.BlockSpec((None, d, tn), lambda i, j: (i, 0, j)),
            pl.BlockSpec((None, 1, tn), lambda i, j: (i, 0, j)),
        ],
        out_specs=pl.BlockSpec((None, MOD_ROWS, tn), lambda i, j: (i, 0, j)),
        out_shape=jax.ShapeDtypeStruct((depth, MOD_ROWS, n), F32),
        compiler_params=_params(("arbitrary", "arbitrary"), 40),
        name="adaln_modulation",
    )(cond, w_mod, b_mod.reshape(depth, 1, n))


def _norm_mod(x, a, b):
    ms = jnp.mean(x * x, axis=-1, keepdims=True)
    return x * lax.rsqrt(ms + EPS) * a + b


def _to_slab(ref, val, rows, p, first=0):
    for j in range(p):
        ref[pl.ds(first * p + j, rows, stride=p), :] = val[:, j * LANES:(j + 1) * LANES]


def _from_slab(ref, rows, p):
    return jnp.concatenate([ref[pl.ds(j, rows, stride=p), :] for j in range(p)], axis=1)


def _pack_bf16_pairs(v):
    half = v.shape[1] // 2
    bits = lax.bitcast_convert_type(v.astype(BF16).astype(F32), jnp.uint32)
    return bits[:, half:] | (bits[:, :half] >> 16)


def _unpack_bf16_pairs(w):
    lo = lax.bitcast_convert_type(w << 16, F32)
    hi = lax.bitcast_convert_type(w & jnp.uint32(0xFFFF0000), F32)
    return jnp.concatenate([lo, hi], axis=1)


def _split_bf16(v):
    hi = v.astype(BF16)
    return hi, (v - hi.astype(F32)).astype(BF16)


def _residual_epilogue(mix, r0, x_ref, gate_ref, a2_ref, b2_ref, wrh_ref, wrl_ref, br_ref, xo_ref, tok_ref, lg_ref):
    tm, d = mix.shape
    xn = x_ref[r0:r0 + tm, :] + gate_ref[...] * mix
    xo_ref[r0:r0 + tm, :] = xn
    tok = _norm_mod(xn, a2_ref[...], b2_ref[...])
    tok_hi, tok_lo = _split_bf16(tok)
    nt = (((1,), (1,)), ((), ()))
    wrh = wrh_ref[...]
    lg = (lax.dot_general(wrh, tok_hi, nt, preferred_element_type=F32)
          + lax.dot_general(wrh, tok_lo, nt, preferred_element_type=F32)
          + lax.dot_general(wrl_ref[...], tok_hi, nt, preferred_element_type=F32))
    lg_ref[:, r0:r0 + tm] = lg + br_ref[...]
    _to_slab(tok_ref, _pack_bf16_pairs(tok), tm, d // (2 * LANES), first=r0)


def _epilogue_specs(tm, d, n_exp, tiles_per_batch):
    bmap = lambda i: (i // tiles_per_batch, 0, 0)
    in_specs = [
        pl.BlockSpec((tm, d), lambda i: (i, 0)),
        pl.BlockSpec((None, 1, d), bmap),
        pl.BlockSpec((None, 1, d), bmap),
        pl.BlockSpec((None, 1, d), bmap),
        _resident((n_exp, d), lambda i: (0, 0)),
        _resident((n_exp, d), lambda i: (0, 0)),
        _resident((n_exp, 1), lambda i: (0, 0)),
    ]
    p = d // (2 * LANES)
    out_specs = [
        pl.BlockSpec((tm, d), lambda i: (i, 0)),
        pl.BlockSpec((tm * p, LANES), lambda i: (i, 0)),
        pl.BlockSpec((n_exp, tm), lambda i: (0, i)),
    ]
    return in_specs, out_specs


def _epilogue_out_shape(n, d, n_exp):
    return [
        jax.ShapeDtypeStruct((n, d), F32),
        jax.ShapeDtypeStruct((n * (d // (2 * LANES)), LANES), jnp.uint32),
        jax.ShapeDtypeStruct((n_exp, n), F32),
    ]


def _router_operands(w_router, b_router):
    wt = w_router.T
    hi = wt.astype(BF16)
    lo = (wt - hi.astype(F32)).astype(BF16)
    return hi, lo, b_router.reshape(-1, 1)


def _ab_in_kernel(x_ref, a_ref, b_ref, w_ref, qg_ref, kg_ref, cos_ref, sin_ref,
                  q_ref, k_ref, v_ref, f_ref):
    h = _norm_mod(x_ref[...], a_ref[...], b_ref[...]).astype(BF16)
    y = jnp.dot(h, w_ref[...], preferred_element_type=F32)
    cos = cos_ref[...]
    sin = sin_ref[...]
    lane = lax.broadcasted_iota(jnp.int32, cos.shape, 1)
    low = (lane % (2 * (ROPE_AXIS_DIM // 2))) < (ROPE_AXIS_DIM // 2)

    def head(col, gain, scale):
        blk = y[:, col:col + HEAD_DIM]
        ms = jnp.mean(blk * blk, axis=-1, keepdims=True)
        r = blk * lax.rsqrt(ms + EPS) * gain
        up = pltpu.roll(r, HEAD_DIM - ROPE_AXIS_DIM // 2, axis=1)
        dn = pltpu.roll(r, ROPE_AXIS_DIM // 2, axis=1)
        return (r * cos + jnp.where(low, up, dn) * sin) * scale

    for hd in range(N_Q_HEADS):
        q_ref[:, hd * HEAD_DIM:(hd + 1) * HEAD_DIM] = head(
            hd * HEAD_DIM, qg_ref[...], HEAD_DIM ** -0.5).astype(BF16)
    for hd in range(N_KV_HEADS):
        k_ref[:, hd * HEAD_DIM:(hd + 1) * HEAD_DIM] = head(
            ATTN_WIDTH + hd * HEAD_DIM, kg_ref[...], 1.0).astype(BF16)
    v_ref[...] = y[:, ATTN_WIDTH + KV_WIDTH:ATTN_WIDTH + 2 * KV_WIDTH].astype(BF16)
    f_ref[...] = y[:, ATTN_WIDTH + 2 * KV_WIDTH:].astype(BF16)


def _ab_in(xin, a_mod, b_mod, w_in, q_gain, k_gain, cos_t, sin_t, seq, tm):
    bsz, lt, d = xin.shape
    n_out = w_in.shape[1]
    lat_tiles = seq // tm
    mmap = lambda b, t: (jnp.where(t < lat_tiles, b, bsz), 0, 0)
    row = lambda b, t: (b, t, 0)
    return pl.pallas_call(
        _ab_in_kernel,
        grid=(bsz, lt // tm),
        in_specs=[
            pl.BlockSpec((None, tm, d), row),
            pl.BlockSpec((None, 1, d), mmap),
            pl.BlockSpec((None, 1, d), mmap),
            _resident((d, n_out), lambda b, t: (0, 0)),
            _resident((1, HEAD_DIM), lambda b, t: (0, 0)),
            _resident((1, HEAD_DIM), lambda b, t: (0, 0)),
            pl.BlockSpec((tm, HEAD_DIM), lambda b, t: (t, 0)),
            pl.BlockSpec((tm, HEAD_DIM), lambda b, t: (t, 0)),
        ],
        out_specs=[
            pl.BlockSpec((None, tm, ATTN_WIDTH), row),
            pl.BlockSpec((None, tm, KV_WIDTH), row),
            pl.BlockSpec((None, tm, KV_WIDTH), row),
            pl.BlockSpec((None, tm, FOURIER_WIDTH), row),
        ],
        out_shape=[
            jax.ShapeDtypeStruct((bsz, lt, ATTN_WIDTH), BF16),
            jax.ShapeDtypeStruct((bsz, lt, KV_WIDTH), BF16),
            jax.ShapeDtypeStruct((bsz, lt, KV_WIDTH), BF16),
            jax.ShapeDtypeStruct((bsz, lt, FOURIER_WIDTH), BF16),
        ],
        compiler_params=_params(("arbitrary", "arbitrary"), 48),
        name="ab_in_proj",
    )(xin, a_mod, b_mod, w_in, q_gain, k_gain, cos_t, sin_t)


def _rope_tables(seq, ctx_len):
    pos = np.arange(seq)
    inv_freq = ROPE_THETA ** (-np.arange(0, ROPE_AXIS_DIM, 2, dtype=np.float32) / ROPE_AXIS_DIM)
    ang_r = (pos // GRID_W).astype(np.float32)[:, None] * inv_freq.astype(np.float32)
    ang_c = (pos % GRID_W).astype(np.float32)[:, None] * inv_freq.astype(np.float32)
    ang_r = jnp.asarray(ang_r, F32)
    ang_c = jnp.asarray(ang_c, F32)
    cos = jnp.concatenate([jnp.cos(ang_r), jnp.cos(ang_r), jnp.cos(ang_c), jnp.cos(ang_c)], axis=1)
    sin = jnp.concatenate([-jnp.sin(ang_r), jnp.sin(ang_r), -jnp.sin(ang_c), jnp.sin(ang_c)], axis=1)
    cos = jnp.concatenate([cos, jnp.ones((ctx_len, HEAD_DIM), F32)], axis=0)
    sin = jnp.concatenate([sin, jnp.zeros((ctx_len, HEAD_DIM), F32)], axis=0)
    return cos, sin


def _attn_kernel(q_ref, k_ref, v_ref, o_ref):
    k = k_ref[...]
    v = v_ref[...]
    for g in range(Q_PER_KV):
        q = q_ref[:, g * HEAD_DIM:(g + 1) * HEAD_DIM]
        s = lax.dot_general(q, k, (((1,), (1,)), ((), ())), preferred_element_type=F32)
        m = jnp.max(s, axis=-1, keepdims=True)
        p = jnp.exp(s - m)
        l = jnp.sum(p, axis=-1, keepdims=True)
        o = jnp.dot(p.astype(BF16), v, preferred_element_type=F32)
        o_ref[:, g * HEAD_DIM:(g + 1) * HEAD_DIM] = (o / l).astype(BF16)


def _attention(q, k, v, seq, tq):
    bsz, lt, _ = q.shape
    gw = Q_PER_KV * HEAD_DIM
    return pl.pallas_call(
        _attn_kernel,
        grid=(bsz, N_KV_HEADS, seq // tq),
        in_specs=[
            pl.BlockSpec((None, tq, gw), lambda b, h, i: (b, i, h)),
            pl.BlockSpec((None, lt, HEAD_DIM), lambda b, h, i: (b, 0, h)),
            pl.BlockSpec((None, lt, HEAD_DIM), lambda b, h, i: (b, 0, h)),
        ],
        out_specs=pl.BlockSpec((None, tq, gw), lambda b, h, i: (b, i, h)),
        out_shape=jax.ShapeDtypeStruct((bsz, seq, ATTN_WIDTH), BF16),
        compiler_params=_params(("arbitrary", "arbitrary", "arbitrary"), 48),
        name="attention",
    )(q, k, v)


def _fourier_chan_kernel(f_ref, cr_ref, ci_ref, zr_ref, zi_ref):
    for g in range(FOURIER_GROUPS):
        cols = slice(g * FOURIER_GROUP_DIM, (g + 1) * FOURIER_GROUP_DIM)
        u = f_ref[:, cols]
        zr_ref[:, cols] = jnp.dot(u, cr_ref[...], preferred_element_type=F32)
        zi_ref[:, cols] = jnp.dot(u, ci_ref[...], preferred_element_type=F32)


def _fourier_outer_kernel(zr_ref, zi_ref, fr_ref, fi_ref, yr_ref, yi_ref):
    fr = fr_ref[...]
    fi = fi_ref[...]
    for jj in range(zr_ref.shape[1]):
        zr = zr_ref[:, jj, :].astype(BF16)
        zi = zi_ref[:, jj, :].astype(BF16)
        yr_ref[:, jj, :] = (jnp.dot(fr, zr, preferred_element_type=F32)
                            - jnp.dot(fi, zi, preferred_element_type=F32))
        yi_ref[:, jj, :] = (jnp.dot(fr, zi, preferred_element_type=F32)
                            + jnp.dot(fi, zr, preferred_element_type=F32))


def _fourier_inner_kernel(yr_ref, yi_ref, twr_ref, twi_ref, gr_ref, gi_ref, o_ref, *, scale):
    gr = gr_ref[...]
    gi = gi_ref[...]
    for kk in range(yr_ref.shape[0]):
        yr = yr_ref[kk]
        yi = yi_ref[kk]
        twr = twr_ref[kk]
        twi = twi_ref[kk]
        tr = (twr * yr - twi * yi).astype(BF16)
        ti = (twr * yi + twi * yr).astype(BF16)
        x = jnp.dot(gr, tr, preferred_element_type=F32) - jnp.dot(gi, ti, preferred_element_type=F32)
        o_ref[:, kk, :] = x * scale


def _dft_parts(n):
    jk = np.outer(np.arange(n), np.arange(n)) % n
    ang = 2 * np.pi * jk / n
    return np.cos(ang), -np.sin(ang)


def _fourier_mix(f, seq):
    bsz = f.shape[0]
    gd = FOURIER_GROUP_DIM
    width = FOURIER_WIDTH
    cr, ci = _dft_parts(gd)
    tm = _row_tile(seq, 512)
    zr, zi = pl.pallas_call(
        _fourier_chan_kernel,
        grid=(bsz, seq // tm),
        in_specs=[
            pl.BlockSpec((None, tm, width), lambda b, i: (b, i, 0)),
            _resident((gd, gd), lambda b, i: (0, 0)),
            _resident((gd, gd), lambda b, i: (0, 0)),
        ],
        out_specs=[pl.BlockSpec((None, tm, width), lambda b, i: (b, i, 0))] * 2,
        out_shape=[jax.ShapeDtypeStruct((bsz, seq, width), F32)] * 2,
        compiler_params=_params(("arbitrary", "arbitrary"), 32),
        name="fourier_channels",
    )(f, jnp.asarray(cr, BF16), jnp.asarray(ci, BF16))

    n_b = min(seq, 64)
    n_a = seq // n_b
    g2 = min(n_b, SUBLANES)
    fr, fi = _dft_parts(n_a)
    split = lambda b, j: (b, 0, j, 0)
    yr, yi = pl.pallas_call(
        _fourier_outer_kernel,
        grid=(bsz, n_b // g2),
        in_specs=[
            pl.BlockSpec((None, n_a, g2, width), split),
            pl.BlockSpec((None, n_a, g2, width), split),
            _resident((n_a, n_a), lambda b, j: (0, 0)),
            _resident((n_a, n_a), lambda b, j: (0, 0)),
        ],
        out_specs=[pl.BlockSpec((None, n_a, g2, width), split)] * 2,
        out_shape=[jax.ShapeDtypeStruct((bsz, n_a, n_b, width), F32)] * 2,
        compiler_params=_params(("arbitrary", "arbitrary"), 40),
        name="fourier_outer_dft",
    )(zr.reshape(bsz, n_a, n_b, width), zi.reshape(bsz, n_a, n_b, width),
      jnp.asarray(fr, BF16), jnp.asarray(fi, BF16))

    ang = 2 * np.pi * np.outer(np.arange(n_a), np.arange(n_b)) / seq
    twr = jnp.asarray(np.cos(ang)[:, :, None], F32)
    twi = jnp.asarray(-np.sin(ang)[:, :, None], F32)
    gr, gi = _dft_parts(n_b)
    g1 = min(n_a, 16)
    out = pl.pallas_call(
        functools.partial(_fourier_inner_kernel, scale=float((seq * gd) ** -0.5)),
        grid=(bsz, n_a // g1),
        in_specs=[
            pl.BlockSpec((None, g1, n_b, width), lambda b, j: (b, j, 0, 0)),
            pl.BlockSpec((None, g1, n_b, width), lambda b, j: (b, j, 0, 0)),
            pl.BlockSpec((g1, n_b, 1), lambda b, j: (j, 0, 0)),
            pl.BlockSpec((g1, n_b, 1), lambda b, j: (j, 0, 0)),
            _resident((n_b, n_b), lambda b, j: (0, 0)),
            _resident((n_b, n_b), lambda b, j: (0, 0)),
        ],
        out_specs=pl.BlockSpec((None, n_b, g1, width), lambda b, j: (b, 0, j, 0)),
        out_shape=jax.ShapeDtypeStruct((bsz, n_b, n_a, width), F32),
        compiler_params=_params(("arbitrary", "arbitrary"), 48),
        name="fourier_inner_dft",
    )(yr, yi, twr, twi, jnp.asarray(gr, BF16), jnp.asarray(gi, BF16))
    return out.reshape(bsz, seq, width)


def _ab_out_kernel(at_ref, fm_ref, wa_ref, wf_ref, x_ref, gate_ref, a2_ref, b2_ref, wrh_ref, wrl_ref, br_ref,
                   xo_ref, tok_ref, lg_ref):
    for r0 in range(0, x_ref.shape[0], SUB_ROWS):
        rows = slice(r0, r0 + SUB_ROWS)
        mix = (jnp.dot(at_ref[rows, :], wa_ref[...], preferred_element_type=F32)
               + jnp.dot(fm_ref[rows, :].astype(BF16), wf_ref[...], preferred_element_type=F32))
        _residual_epilogue(mix, r0, x_ref, gate_ref, a2_ref, b2_ref, wrh_ref, wrl_ref, br_ref,
                           xo_ref, tok_ref, lg_ref)


def _ab_out(attn, fmix, w_attn, w_four, x, gate, a2, b2, router, seq, tm):
    n, d = x.shape
    n_exp = router[0].shape[0]
    ep_in, ep_out = _epilogue_specs(tm, d, n_exp, seq // tm)
    return pl.pallas_call(
        _ab_out_kernel,
        grid=(n // tm,),
        in_specs=[
            pl.BlockSpec((tm, ATTN_WIDTH), lambda i: (i, 0)),
            pl.BlockSpec((tm, FOURIER_WIDTH), lambda i: (i, 0)),
            _resident((ATTN_WIDTH, d), lambda i: (0, 0)),
            _resident((FOURIER_WIDTH, d), lambda i: (0, 0)),
        ] + ep_in,
        out_specs=ep_out,
        out_shape=_epilogue_out_shape(n, d, n_exp),
        compiler_params=_params(("arbitrary",), 56),
        name="ab_out_proj",
    )(attn, fmix, w_attn, w_four, x, gate, a2, b2, *router)


def _cv_in_kernel(x_ref, a_ref, b_ref, w_ref, bias_ref, u_ref):
    d = x_ref.shape[1]
    h = _norm_mod(x_ref[...], a_ref[...], b_ref[...]).astype(BF16)
    y = jnp.dot(h, w_ref[...], preferred_element_type=F32) + bias_ref[...]
    u_ref[...] = y[:, :d] * jax.nn.sigmoid(y[:, d:])


def _cv_in(x, a_mod, b_mod, w_in, b_in, seq, tm):
    n, d = x.shape
    tiles = seq // tm
    bmap = lambda i: (i // tiles, 0, 0)
    return pl.pallas_call(
        _cv_in_kernel,
        grid=(n // tm,),
        in_specs=[
            pl.BlockSpec((tm, d), lambda i: (i, 0)),
            pl.BlockSpec((None, 1, d), bmap),
            pl.BlockSpec((None, 1, d), bmap),
            _resident((d, 2 * d), lambda i: (0, 0)),
            _resident((1, 2 * d), lambda i: (0, 0)),
        ],
        out_specs=pl.BlockSpec((tm, d), lambda i: (i, 0)),
        out_shape=jax.ShapeDtypeStruct((n, d), F32),
        compiler_params=_params(("arbitrary",), 48),
        name="conv_in_proj_glu",
    )(x, a_mod, b_mod, w_in, b_in)


def _dwconv_kernel(prev_ref, cur_ref, next_ref, w_ref, b_ref, o_ref, win_ref, *, window, row_chunk):
    i = pl.program_id(1)
    tm, tc = cur_ref.shape
    first = i == 0
    last = i == pl.num_programs(1) - 1
    win_ref[0:HALO, :] = jnp.where(first, 0.0, prev_ref[...])
    win_ref[HALO:HALO + tm, :] = cur_ref[...]
    win_ref[HALO + tm:, :] = jnp.where(last, 0.0, next_ref[...])
    lead = HALO - window // 2
    ext = row_chunk + SUBLANES
    assert tm % row_chunk == 0 and lead + window - 1 < 4 * SUBLANES and ext + 3 * SUBLANES <= row_chunk + 2 * HALO
    for c in range(tc // LANES):
        cols = slice(c * LANES, (c + 1) * LANES)
        for r in range(tm // row_chunk):
            r0 = r * row_chunk
            acc = jnp.zeros((row_chunk, LANES), F32) + b_ref[:, cols]
            for b in range(SUBLANES):
                part = None
                for a in range(4):
                    j = SUBLANES * a + b - lead
                    if 0 <= j < window:
                        lo = r0 + SUBLANES * a
                        term = w_ref[j:j + 1, cols] * win_ref[lo:lo + ext, cols]
                        part = term if part is None else part + term
                acc = acc + part[b:b + row_chunk]
            o_ref[r0:r0 + row_chunk, cols] = acc


def _dwconv(u, w_dw, b_dw, tm, tc):
    bsz, seq, d = u.shape
    window = w_dw.shape[0]
    assert window // 2 <= HALO and tm % HALO == 0
    hb = tm // HALO
    n_halo = seq // HALO
    return pl.pallas_call(
        functools.partial(_dwconv_kernel, window=window, row_chunk=min(tm, 64)),
        grid=(bsz, seq // tm, d // tc),
        in_specs=[
            pl.BlockSpec((None, HALO, tc), lambda b, i, c: (b, jnp.maximum(i * hb - 1, 0), c)),
            pl.BlockSpec((None, tm, tc), lambda b, i, c: (b, i, c)),
            pl.BlockSpec((None, HALO, tc), lambda b, i, c: (b, jnp.minimum((i + 1) * hb, n_halo - 1), c)),
            pl.BlockSpec((window, tc), lambda b, i, c: (0, c)),
            pl.BlockSpec((1, tc), lambda b, i, c: (0, c)),
        ],
        out_specs=pl.BlockSpec((None, tm, tc), lambda b, i, c: (b, i, c)),
        out_shape=jax.ShapeDtypeStruct((bsz, seq, d), F32),
        scratch_shapes=[pltpu.VMEM((tm + 2 * HALO, tc), F32)],
        compiler_params=_params(("arbitrary", "arbitrary", "arbitrary"), 32),
        name="depthwise_conv",
    )(u, u, u, w_dw, b_dw)


def _cv_out_kernel(v_ref, lng_ref, lnb_ref, w_ref, bo_ref, x_ref, gate_ref, a2_ref, b2_ref, wrh_ref, wrl_ref, br_ref,
                   xo_ref, tok_ref, lg_ref):
    for r0 in range(0, x_ref.shape[0], SUB_ROWS):
        v = v_ref[r0:r0 + SUB_ROWS, :]
        mu = jnp.mean(v, axis=-1, keepdims=True)
        vc = v - mu
        var = jnp.mean(vc * vc, axis=-1, keepdims=True)
        y = vc * lax.rsqrt(var + EPS) * lng_ref[...] + lnb_ref[...]
        y = (y * jax.nn.sigmoid(y)).astype(BF16)
        mix = jnp.dot(y, w_ref[...], preferred_element_type=F32) + bo_ref[...]
        _residual_epilogue(mix, r0, x_ref, gate_ref, a2_ref, b2_ref, wrh_ref, wrl_ref, br_ref,
                           xo_ref, tok_ref, lg_ref)


def _cv_out(v, ln_g, ln_b, w_out, b_out, x, gate, a2, b2, router, seq, tm):
    n, d = x.shape
    n_exp = router[0].shape[0]
    ep_in, ep_out = _epilogue_specs(tm, d, n_exp, seq // tm)
    return pl.pallas_call(
        _cv_out_kernel,
        grid=(n // tm,),
        in_specs=[
            pl.BlockSpec((tm, d), lambda i: (i, 0)),
            _resident((1, d), lambda i: (0, 0)),
            _resident((1, d), lambda i: (0, 0)),
            _resident((d, d), lambda i: (0, 0)),
            _resident((1, d), lambda i: (0, 0)),
        ] + ep_in,
        out_specs=ep_out,
        out_shape=_epilogue_out_shape(n, d, n_exp),
        compiler_params=_params(("arbitrary",), 56),
        name="conv_out_proj",
    )(v, ln_g, ln_b, w_out, b_out, x, gate, a2, b2, *router)


def _route_kernel(lg_ref, idx_ref, gate_ref, rank_ref, cnt_ref, run_ref):
    i = pl.program_id(0)

    @pl.when(i == 0)
    def _():
        run_ref[...] = jnp.zeros_like(run_ref)

    lg = lg_ref[...]
    n_exp, tr = lg.shape
    sub = lax.broadcasted_iota(jnp.int32, (n_exp, tr), 0).astype(F32)
    work = lg
    vals, idxs = [], []
    member = jnp.zeros((n_exp, tr), F32)
    for _ in range(TOP_K):
        m = jnp.max(work, axis=0, keepdims=True)
        sel = jnp.min(jnp.where(work == m, sub, float(n_exp)), axis=0, keepdims=True)
        hit = sub == sel
        vals.append(m)
        idxs.append(sel)
        member = jnp.where(hit, 1.0, member)
        work = jnp.where(hit, -jnp.inf, work)
    es = [jnp.exp(v - vals[0]) for v in vals]
    tot = es[0] + es[1] + es[2] + es[3]
    r_i = lax.broadcasted_iota(jnp.int32, (tr, tr), 0)
    c_i = lax.broadcasted_iota(jnp.int32, (tr, tr), 1)
    earlier = (r_i < c_i).astype(BF16)
    before = jnp.dot(member.astype(BF16), earlier, preferred_element_type=F32) + run_ref[...]
    for k in range(TOP_K):
        idx_ref[k:k + 1, :] = idxs[k].astype(jnp.int32)
        gate_ref[k:k + 1, :] = es[k] / tot
        rank_ref[k:k + 1, :] = jnp.sum(jnp.where(sub == idxs[k], before, 0.0),
                                       axis=0, keepdims=True).astype(jnp.int32)
    run_ref[...] += jnp.sum(member, axis=1, keepdims=True)
    cnt_ref[...] = run_ref[...].astype(jnp.int32)


def _route(logits_t, tr):
    n_exp, n = logits_t.shape
    tok = lambda i: (0, i)
    return pl.pallas_call(
        _route_kernel,
        grid=(n // tr,),
        in_specs=[pl.BlockSpec((n_exp, tr), tok)],
        out_specs=[
            pl.BlockSpec((TOP_K, tr), tok),
            pl.BlockSpec((TOP_K, tr), tok),
            pl.BlockSpec((TOP_K, tr), tok),
            pl.BlockSpec((n_exp, 1), lambda i: (0, 0)),
        ],
        out_shape=[
            jax.ShapeDtypeStruct((TOP_K, n), jnp.int32),
            jax.ShapeDtypeStruct((TOP_K, n), F32),
            jax.ShapeDtypeStruct((TOP_K, n), jnp.int32),
            jax.ShapeDtypeStruct((n_exp, 1), jnp.int32),
        ],
        scratch_shapes=[pltpu.VMEM((n_exp, 1), F32)],
        compiler_params=_params(("arbitrary",), 32),
        name="moe_route",
    )(logits_t)


def _w1_split_kernel(w_ref, perm_ref, g_ref, l_ref):
    wide = 2 * LANES
    for c in range(w_ref.shape[1] // wide):
        t = jnp.dot(w_ref[:, c * wide:(c + 1) * wide].astype(BF16), perm_ref[...], preferred_element_type=F32)
        g_ref[:, c * LANES:(c + 1) * LANES] = t[:, :LANES].astype(BF16)
        l_ref[:, c * LANES:(c + 1) * LANES] = t[:, LANES:].astype(BF16)


def _w1_split(w1_all, layer):
    _, n_exp, d, ff2 = w1_all.shape
    ff = ff2 // 2
    assert ff % LANES == 0
    perm = np.zeros((2 * LANES, 2 * LANES), np.float32)
    perm[2 * np.arange(LANES), np.arange(LANES)] = 1.0
    perm[2 * np.arange(LANES) + 1, LANES + np.arange(LANES)] = 1.0
    tk = _row_tile(d, 512)
    return pl.pallas_call(
        _w1_split_kernel,
        grid=(n_exp, d // tk),
        in_specs=[
            pl.BlockSpec((None, None, tk, ff2), lambda e, r: (layer, e, r, 0)),
            _resident((2 * LANES, 2 * LANES), lambda e, r: (0, 0)),
        ],
        out_specs=[pl.BlockSpec((None, tk, ff), lambda e, r: (e, r, 0))] * 2,
        out_shape=[jax.ShapeDtypeStruct((n_exp, d, ff), BF16)] * 2,
        compiler_params=_params(("arbitrary", "arbitrary"), 32),
        name="moe_w1_split",
    )(w1_all, jnp.asarray(perm, BF16))


def _dispatch_kernel(pad_lo_ref, pad_n_ref, tail_ref, dest_ref, tok_ref, xs_ref, tbuf_ref, zero_ref,
                     lsem, rsem, zsem, *, p, n_exp):
    i = pl.program_id(0)
    n_steps = pl.num_programs(0)
    tt = dest_ref.shape[0] // TOP_K
    rows = tt * p
    blk = zero_ref.shape[0]
    slot = i % 2

    def load(step, s):
        first = pl.multiple_of(step * rows, rows)
        return pltpu.make_async_copy(tok_ref.at[pl.ds(first, rows), :], tbuf_ref.at[s], lsem.at[s])

    def rows_wait(s):
        for _ in range(TOP_K):
            pltpu.make_async_copy(tbuf_ref.at[s], xs_ref.at[pl.ds(0, rows), :], rsem.at[s]).wait()

    def zero_row(row):
        return pltpu.make_async_copy(zero_ref.at[pl.ds(0, p), :], xs_ref.at[pl.ds(row * p, p), :], zsem)

    def zero_block(b):
        return pltpu.make_async_copy(zero_ref, xs_ref.at[pl.ds(b * blk, blk), :], zsem)

    @pl.when(i == 0)
    def _():
        load(0, 0).start()

    load(i, slot).wait()

    def issue(t, c):
        for k in range(TOP_K):
            pltpu.make_async_copy(tbuf_ref.at[slot, pl.ds(t * p, p), :],
                                  xs_ref.at[pl.ds(dest_ref[t * TOP_K + k] * p, p), :], rsem.at[slot]).start()
        return c

    lax.fori_loop(0, tt, issue, 0)

    @pl.when(i == 0)
    def _():
        zero_ref[...] = jnp.zeros_like(zero_ref)
        for e in range(n_exp):
            lo = pad_lo_ref[e]
            lax.fori_loop(0, pad_n_ref[e], lambda j, c: (zero_row(lo + j).start(), c)[1], 0)
        first = tail_ref[0]
        lax.fori_loop(0, tail_ref[1], lambda j, c: (zero_block(first + j).start(), c)[1], 0)
        for e in range(n_exp):
            lo = pad_lo_ref[e]
            lax.fori_loop(0, pad_n_ref[e], lambda j, c: (zero_row(lo + j).wait(), c)[1], 0)
        lax.fori_loop(0, tail_ref[1], lambda j, c: (zero_block(first + j).wait(), c)[1], 0)

    @pl.when(i > 0)
    def _():
        rows_wait(1 - slot)

    @pl.when(i + 1 < n_steps)
    def _():
        load(i + 1, 1 - slot).start()

    @pl.when(i == n_steps - 1)
    def _():
        rows_wait(slot)


def _dispatch(pad_lo, pad_n, tail, dest_flat, tok_slab, n_rows, n, tt, tme):
    p = tok_slab.shape[0] // n
    n_exp = pad_lo.shape[0]
    return pl.pallas_call(
        functools.partial(_dispatch_kernel, p=p, n_exp=n_exp),
        grid_spec=pltpu.PrefetchScalarGridSpec(
            num_scalar_prefetch=3,
            grid=(n // tt,),
            in_specs=[
                pl.BlockSpec((tt * TOP_K,), lambda i, lo, pn, tl: (i,), memory_space=pltpu.SMEM),
                pl.BlockSpec(memory_space=pl.ANY),
            ],
            out_specs=pl.BlockSpec(memory_space=pl.ANY),
            scratch_shapes=[pltpu.VMEM((2, tt * p, LANES), tok_slab.dtype),
                            pltpu.VMEM((tme * p, LANES), tok_slab.dtype),
                            pltpu.SemaphoreType.DMA((2,)), pltpu.SemaphoreType.DMA((2,)),
                            pltpu.SemaphoreType.DMA],
        ),
        out_shape=jax.ShapeDtypeStruct((n_rows * p, LANES), tok_slab.dtype),
        compiler_params=_params(("arbitrary",), 32),
        name="moe_dispatch",
    )(pad_lo, pad_n, tail, dest_flat, tok_slab)


def _expert_kernel(be_ref, nu_ref, xs_ref, w1g_ref, w1l_ref, b1g_ref, b1l_ref, w2_ref, b2_ref, y_ref, w2b_ref, *, p):
    px = p // 2
    i = pl.program_id(0)

    @pl.when((i == 0) | (be_ref[i] != be_ref[jnp.maximum(i - 1, 0)]))
    def _():
        w2b_ref[...] = w2_ref[...].astype(BF16)

    @pl.when(i < nu_ref[0])
    def _():
        tme = xs_ref.shape[0] // px
        x = _unpack_bf16_pairs(_from_slab(xs_ref, tme, px)).astype(BF16)
        hg = jnp.dot(x, w1g_ref[...], preferred_element_type=F32) + b1g_ref[...]
        hl = jnp.dot(x, w1l_ref[...], preferred_element_type=F32) + b1l_ref[...]
        g = jnp.minimum(hg, SWIGLU_LIMIT)
        lin = jnp.clip(hl, -SWIGLU_LIMIT, SWIGLU_LIMIT)
        act = (g * jax.nn.sigmoid(SWIGLU_ALPHA * g) * (lin + 1.0)).astype(BF16)
        y = jnp.dot(act, w2b_ref[...], preferred_element_type=F32) + b2_ref[...]
        _to_slab(y_ref, _pack_bf16_pairs(y), tme, px)

    @pl.when(i >= nu_ref[0])
    def _():
        y_ref[...] = jnp.zeros_like(y_ref)


def _experts(block_e, n_used, xs, w1g, w1l, b1g, b1l, w2_all, layer, b2, tme):
    n_exp, d, ff = w1g.shape
    p = d // LANES
    px = p // 2
    n_rows = xs.shape[0] // px
    rows_in = lambda i, be, nu: (jnp.minimum(i, nu[0] - 1), 0)
    wsel = lambda i, be, nu: (be[i], 0, 0)
    return pl.pallas_call(
        functools.partial(_expert_kernel, p=p),
        grid_spec=pltpu.PrefetchScalarGridSpec(
            num_scalar_prefetch=2,
            grid=(n_rows // tme,),
            in_specs=[
                pl.BlockSpec((tme * px, LANES), rows_in),
                pl.BlockSpec((None, d, ff), wsel),
                pl.BlockSpec((None, d, ff), wsel),
                pl.BlockSpec((None, 1, ff), wsel),
                pl.BlockSpec((None, 1, ff), wsel),
                pl.BlockSpec((None, None, ff, d), lambda i, be, nu: (layer, be[i], 0, 0)),
                pl.BlockSpec((None, 1, d), wsel),
            ],
            out_specs=pl.BlockSpec((tme * px, LANES), lambda i, be, nu: (i, 0)),
            scratch_shapes=[pltpu.VMEM((ff, d), BF16)],
        ),
        out_shape=jax.ShapeDtypeStruct((n_rows * px, LANES), jnp.uint32),
        compiler_params=_params(("arbitrary",), 58),
        name="moe_experts",
    )(block_e, n_used, xs, w1g, w1l, b1g, b1l, w2_all, b2)


def _combine_kernel(dest_ref, dnext_ref, y_ref, gates_ref, x_ref, gate2_ref, gfin_ref, o_ref, buf_ref, sem,
                    *, p, final):
    i = pl.program_id(0)
    tt = x_ref.shape[0]
    slot = i % 2

    def fetch(d_ref, s):
        def body(t, c):
            for k in range(TOP_K):
                pltpu.make_async_copy(y_ref.at[pl.ds(d_ref[t * TOP_K + k] * p, p), :],
                                      buf_ref.at[s, k, pl.ds(t * p, p), :], sem.at[s]).start()
            return c
        lax.fori_loop(0, tt, body, 0)

    @pl.when(i == 0)
    def _():
        fetch(dest_ref, 0)

    @pl.when(i + 1 < pl.num_programs(0))
    def _():
        fetch(dnext_ref, 1 - slot)

    for k in range(TOP_K):
        pltpu.make_async_copy(y_ref.at[pl.ds(0, tt * p), :], buf_ref.at[slot, k], sem.at[slot]).wait()
    gates = gates_ref[...]
    acc = gates[:, 0:1] * _unpack_bf16_pairs(_from_slab(buf_ref.at[slot, 0], tt, p))
    for k in range(1, TOP_K):
        acc = acc + gates[:, k:k + 1] * _unpack_bf16_pairs(_from_slab(buf_ref.at[slot, k], tt, p))
    xn = x_ref[...] + gate2_ref[...] * acc
    if final:
        ms = jnp.mean(xn * xn, axis=-1, keepdims=True)
        xn = xn * lax.rsqrt(ms + EPS) * gfin_ref[...]
    o_ref[...] = xn


def _combine(dest_flat, y_slab, gates, x, gate2, g_final, seq, tt, final):
    n, d = x.shape
    p = d // (2 * LANES)
    tiles = seq // tt
    n_tiles = n // tt
    return pl.pallas_call(
        functools.partial(_combine_kernel, p=p, final=final),
        grid=(n_tiles,),
        in_specs=[
            pl.BlockSpec((tt * TOP_K,), lambda i: (i,), memory_space=pltpu.SMEM),
            pl.BlockSpec((tt * TOP_K,), lambda i: (jnp.minimum(i + 1, n_tiles - 1),), memory_space=pltpu.SMEM),
            pl.BlockSpec(memory_space=pl.ANY),
            pl.BlockSpec((tt, TOP_K), lambda i: (i, 0)),
            pl.BlockSpec((tt, d), lambda i: (i, 0)),
            pl.BlockSpec((None, 1, d), lambda i: (i // tiles, 0, 0)),
            pl.BlockSpec((1, d), lambda i: (0, 0)),
        ],
        out_specs=pl.BlockSpec((tt, d), lambda i: (i, 0)),
        out_shape=jax.ShapeDtypeStruct((n, d), F32),
        scratch_shapes=[pltpu.VMEM((2, TOP_K, tt * p, LANES), jnp.uint32), pltpu.SemaphoreType.DMA((2,))],
        compiler_params=_params(("arbitrary",), 40),
        name="moe_combine",
    )(dest_flat, dest_flat, y_slab, gates, x, gate2, g_final)


def _moe(x, tok_slab, logits_t, gate2, w1_all, b1, w2_all, b2, layer, g_final, seq, final):
    n, d = x.shape
    n_exp = w1_all.shape[1]
    ff = w2_all.shape[2]
    tme = 256
    top_idx, gates, rank, counts = _route(logits_t, _row_tile(n, 256))
    counts = counts[:, 0]
    padded = (counts + tme - 1) // tme * tme
    pad_end = jnp.cumsum(padded)
    pad_start = pad_end - padded
    experts = jnp.arange(n_exp, dtype=jnp.int32)[:, None, None]
    start = jnp.sum(jnp.where(top_idx[None] == experts, pad_start[:, None, None], 0), axis=0)
    dest = (start + rank).T.reshape(-1).astype(jnp.int32)
    n_rows = (n * TOP_K + tme - 1) // tme * tme + n_exp * tme
    n_blocks = n_rows // tme
    blk_start = jnp.arange(n_blocks, dtype=jnp.int32) * tme
    block_e = jnp.minimum(jnp.sum(blk_start[:, None] >= pad_end[None, :], axis=1), n_exp - 1).astype(jnp.int32)
    n_used = (pad_end[-1] // tme).astype(jnp.int32).reshape(1)
    pad_lo = (pad_start + counts).astype(jnp.int32)
    pad_n = (padded - counts).astype(jnp.int32)
    tail = jnp.concatenate([n_used, n_blocks - n_used]).astype(jnp.int32)

    xs = _dispatch(pad_lo, pad_n, tail, dest, tok_slab, n_rows, n, _row_tile(n, 256), tme)
    w1g, w1l = _w1_split(w1_all, layer)
    b1g = b1[:, 0::2].reshape(n_exp, 1, ff)
    b1l = b1[:, 1::2].reshape(n_exp, 1, ff)
    y = _experts(block_e, n_used, xs, w1g, w1l, b1g, b1l, w2_all, layer, b2.reshape(n_exp, 1, d), tme)
    return _combine(dest, y, gates.T, x, gate2, g_final.reshape(1, d), seq, _row_tile(seq, 128), final)


def kernel(x, c, ctx, c_ctx, w_mod, b_mod, g_mix, g_ffn, ab_w_in, ab_q_gain, ab_k_gain, ab_w_out,
           cv_w_in, cv_b_in, cv_w_dw, cv_b_dw, cv_ln_g, cv_ln_b, cv_w_out, cv_b_out,
           moe_w_router, moe_b_router, moe_w1, moe_b1, moe_w2, moe_b2, g_final):
    bsz, seq, d = x.shape
    ctx_len = ctx.shape[1]
    depth = w_mod.shape[0]
    n = bsz * seq
    assert bsz + 1 <= MOD_ROWS and d % LANES == 0
    tm = _row_tile(seq, 256)
    tm_out = _row_tile(seq, 2 * SUB_ROWS)
    assert ctx_len % tm == 0

    cond = jnp.zeros((MOD_ROWS, d), F32).at[:bsz].set(c).at[bsz].set(c_ctx)
    mod = _modulation(cond, w_mod, b_mod).reshape(depth, MOD_ROWS, N_MOD, d)

    xf = x.reshape(n, d)
    for i in range(depth):
        j = i // 2
        sh1, sc1, gt1, sh2, sc2, gt2 = (mod[i, :, m].reshape(MOD_ROWS, 1, d) for m in range(N_MOD))
        a1 = g_mix[i] * (1.0 + sc1)
        a2 = g_ffn[i] * (1.0 + sc2)
        router = _router_operands(moe_w_router[i], moe_b_router[i])
        if i % 2 == 0:
            assert not any(k % 2 == 0 for k in range(i + 1, depth)), "context update path not needed at this depth"
            xin = jnp.concatenate([xf.reshape(bsz, seq, d), ctx], axis=1)
            cos_t, sin_t = _rope_tables(seq, ctx_len)
            q, k, v, f = _ab_in(xin, a1, sh1, ab_w_in[j].astype(BF16), ab_q_gain[j].reshape(1, -1),
                                ab_k_gain[j].reshape(1, -1), cos_t, sin_t, seq, tm)
            attn = _attention(q, k, v, seq, tm)
            fmix = _fourier_mix(f, seq)
            w_out = ab_w_out[j].astype(BF16)
            xf, tok, logits = _ab_out(attn.reshape(n, ATTN_WIDTH), fmix.reshape(n, FOURIER_WIDTH),
                                      w_out[:ATTN_WIDTH], w_out[ATTN_WIDTH:], xf, gt1, a2, sh2,
                                      router, seq, tm_out)
        else:
            u = _cv_in(xf, a1, sh1, cv_w_in[j].astype(BF16), cv_b_in[j].reshape(1, -1), seq, tm)
            v = _dwconv(u.reshape(bsz, seq, d), cv_w_dw[j], cv_b_dw[j].reshape(1, -1),
                        _row_tile(seq, 128), _row_tile(d, 512))
            xf, tok, logits = _cv_out(v.reshape(n, d), cv_ln_g[j].reshape(1, -1), cv_ln_b[j].reshape(1, -1),
                                      cv_w_out[j].astype(BF16), cv_b_out[j].reshape(1, -1), xf, gt1, a2, sh2,
                                      router, seq, tm_out)
        xf = _moe(xf, tok, logits, gt2, moe_w1, moe_b1[i], moe_w2, moe_b2[i], i, g_final, seq,
                  final=(i == depth - 1))
    return xf.reshape(bsz, seq, d)
```
